```python
import math
import jax, jax.numpy as jnp
from jax import lax
import numpy as np

D_MODEL = 1024
BATCH = 16
SEQ = 2048
DEPTH = 2
DEC_BATCH = 2
DEC_SEQ = 8192
PAST_LEN = 128

GRID_W = 64
HEAD_DIM = 64
ATT_HEADS = 8
ATT_KV_HEADS = 2
ATT_GROUPS = ATT_HEADS // ATT_KV_HEADS
Q_BLOCK = 128
ROPE_THETA = 10000.0
GLA_HEADS = 4
GLA_DK = 64
GLA_DV = 128
GLA_GATE_RANK = 16
GLA_GATE_TAU = 16.0
GLA_CHUNK = 64
RET_HEADS = 4
RET_DK = 64
RET_DV = 128
RET_CHUNK = 128
BRANCH_W = 512
N_BRANCH = 3
D_FF = 4 * D_MODEL
EPS = 1e-6

IN_SPLITS = (ATT_HEADS * HEAD_DIM, ATT_KV_HEADS * HEAD_DIM, ATT_KV_HEADS * HEAD_DIM,
             GLA_HEADS * GLA_DK, GLA_HEADS * GLA_DK, GLA_HEADS * GLA_DV, 2 * GLA_GATE_RANK, GLA_HEADS * GLA_DV,
             RET_HEADS * RET_DK, RET_HEADS * RET_DK, RET_HEADS * RET_DV, RET_HEADS * RET_DV)
IN_WIDTH = sum(IN_SPLITS)

kernel_name = "hybrid_gqa_gla_retention_encoder"


def rms_norm(x, g):
    xf = x.astype(jnp.float32)
    y = xf * lax.rsqrt(jnp.mean(xf * xf, axis=-1, keepdims=True) + EPS)
    return (y * g.astype(jnp.float32)).astype(x.dtype)


def head_group_norm(x):
    xf = x.astype(jnp.float32)
    mu = jnp.mean(xf, axis=-1, keepdims=True)
    xc = xf - mu
    return xc * lax.rsqrt(jnp.mean(xc * xc, axis=-1, keepdims=True) + EPS)


def split_columns(p):
    out, start = [], 0
    for w in IN_SPLITS:
        out.append(p[..., start:start + w])
        start += w
    return out


def axial_rope_tables(seq_len):
    rows = seq_len // GRID_W
    row_ids = jnp.repeat(jnp.arange(rows), GRID_W).astype(jnp.float32)
    col_ids = jnp.tile(jnp.arange(GRID_W), rows).astype(jnp.float32)
    axis_dim = HEAD_DIM // 2
    inv = ROPE_THETA ** (-jnp.arange(0, axis_dim, 2, dtype=jnp.float32) / axis_dim)
    ang_r = row_ids[:, None] * inv[None, :]
    ang_c = col_ids[:, None] * inv[None, :]
    ang_r = jnp.concatenate([ang_r, ang_r], axis=-1)
    ang_c = jnp.concatenate([ang_c, ang_c], axis=-1)
    return (jnp.cos(ang_r), jnp.sin(ang_r), jnp.cos(ang_c), jnp.sin(ang_c))


def _rotate(x, cos, sin):
    x1, x2 = jnp.split(x, 2, axis=-1)
    rot = jnp.concatenate([-x2, x1], axis=-1)
    return x * cos[:, None, :] + rot * sin[:, None, :]


def apply_axial_rope(x, rope):
    cos_r, sin_r, cos_c, sin_c = rope
    xr, xc = jnp.split(x, 2, axis=-1)
    return jnp.concatenate([_rotate(xr, cos_r, sin_r), _rotate(xc, cos_c, sin_c)], axis=-1).astype(x.dtype)


def gqa_attention(q, k, v):
    B, S = q.shape[0], q.shape[1]
    nb = S // Q_BLOCK
    qb = q.reshape(B, nb, Q_BLOCK, ATT_KV_HEADS, ATT_GROUPS, HEAD_DIM).transpose(1, 0, 2, 3, 4, 5)
    scale = HEAD_DIM ** -0.5

    def block(qi):
        s = jnp.einsum('bqhgd,bkhd->bhgqk', qi, k, preferred_element_type=jnp.float32) * scale
        p = jax.nn.softmax(s, axis=-1)
        return jnp.einsum('bhgqk,bkhd->bqhgd', p.astype(v.dtype), v)

    o = lax.map(block, qb)
    return o.transpose(1, 0, 2, 3, 4, 5).reshape(B, S, ATT_HEADS * HEAD_DIM)


def gla_scan(q, k, v, log_a):
    B, H, S, dk = q.shape
    dv = v.shape[-1]
    C = GLA_CHUNK
    n = S // C
    chunk = lambda t: t.reshape(B, H, n, C, t.shape[-1]).transpose(2, 0, 1, 3, 4)
    causal = jnp.tril(jnp.ones((C, C), dtype=bool))
    mid = C // 2 - 1

    def step(state, inp):
        qi, ki, vi, ai = inp
        qf, kf, vf = qi.astype(jnp.float32), ki.astype(jnp.float32), vi.astype(jnp.float32)
        b = jnp.cumsum(ai, axis=-2)
        b_ref = b[..., mid:mid + 1, :]
        q_in = qf * jnp.exp(b - b_ref)
        k_in = kf * jnp.exp(b_ref - b)
        s = jnp.where(causal, jnp.einsum('bhid,bhjd->bhij', q_in, k_in), 0.0)
        intra = jnp.einsum('bhij,bhjv->bhiv', s, vf)
        inter = jnp.einsum('bhid,bhdv->bhiv', qf * jnp.exp(b), state)
        b_last = b[..., -1:, :]
        new = state * jnp.exp(b_last[..., 0, :])[..., :, None] + jnp.einsum(
            'bhjd,bhjv->bhdv', kf * jnp.exp(b_last - b), vf)
        return new, intra + inter

    state0 = jnp.zeros((B, H, dk, dv), jnp.float32)
    _, o = lax.scan(step, state0, (chunk(q), chunk(k), chunk(v), chunk(log_a)))
    return o.transpose(1, 2, 0, 3, 4).reshape(B, H, S, dv)


def retention_scan(q, k, v, log_gamma):
    B, H, S, dk = q.shape
    dv = v.shape[-1]
    C = RET_CHUNK
    n = S // C
    chunk = lambda t: t.reshape(B, H, n, C, t.shape[-1]).transpose(2, 0, 1, 3, 4)
    idx = jnp.arange(C, dtype=jnp.float32)
    diff = idx[:, None] - idx[None, :]
    lg = log_gamma[:, None, None]
    decay_mask = jnp.where(diff >= 0, jnp.exp(lg * jnp.maximum(diff, 0.0)), 0.0)
    q_decay = jnp.exp(log_gamma[:, None] * (idx + 1.0))[:, :, None]
    k_decay = jnp.exp(log_gamma[:, None] * (C - 1.0 - idx))[:, :, None]
    chunk_decay = jnp.exp(log_gamma * C)[:, None, None]

    def step(state, inp):
        qi, ki, vi = inp
        qf, kf, vf = qi.astype(jnp.float32), ki.astype(jnp.float32), vi.astype(jnp.float32)
        s = jnp.einsum('bhid,bhjd->bhij', qf, kf) * decay_mask
        intra = jnp.einsum('bhij,bhjv->bhiv', s, vf)
        inter = jnp.einsum('bhid,bhdv->bhiv', qf, state) * q_decay
        new = state * chunk_decay + jnp.einsum('bhjd,bhjv->bhdv', kf * k_decay, vf)
        return new, intra + inter

    state0 = jnp.zeros((B, H, dk, dv), jnp.float32)
    _, o = lax.scan(step, state0, (chunk(q), chunk(k), chunk(v)))
    return o.transpose(1, 2, 0, 3, 4).reshape(B, H, S, dv)


def _flip(t):
    return jnp.flip(t, axis=2)


def mixer(xn, w_in, q_gain, k_gain, gla_w_gate, gla_b_gate, gla_gain, w_branch, w_merge, w_out, rope):
    B, S, D = xn.shape
    f32 = jnp.float32
    aq, ak, av, gq, gk, gv, ga, gg, rq, rk, rv, rg = split_columns(xn @ w_in)
    to_heads = lambda t, h: t.reshape(B, S, h, -1).transpose(0, 2, 1, 3)

    q = apply_axial_rope(rms_norm(aq.reshape(B, S, ATT_HEADS, HEAD_DIM), q_gain), rope)
    k = apply_axial_rope(rms_norm(ak.reshape(B, S, ATT_KV_HEADS, HEAD_DIM), k_gain), rope)
    v = av.reshape(B, S, ATT_KV_HEADS, HEAD_DIM)
    o_att = gqa_attention(q, k, v)

    gq_h = to_heads(gq, GLA_HEADS) * (GLA_DK ** -0.5)
    gk_h = to_heads(gk, GLA_HEADS)
    gv_h = to_heads(gv, GLA_HEADS)
    z = ga.astype(f32)
    la_f = jax.nn.log_sigmoid(z[..., :GLA_GATE_RANK] @ gla_w_gate[0].astype(f32)
                              + gla_b_gate[0].astype(f32)) / GLA_GATE_TAU
    la_b = jax.nn.log_sigmoid(z[..., GLA_GATE_RANK:] @ gla_w_gate[1].astype(f32)
                              + gla_b_gate[1].astype(f32)) / GLA_GATE_TAU
    la_f, la_b = to_heads(la_f, GLA_HEADS), to_heads(la_b, GLA_HEADS)
    o_g = gla_scan(gq_h, gk_h, gv_h, la_f) + _flip(
        gla_scan(_flip(gq_h), _flip(gk_h), _flip(gv_h), _flip(la_b)))
    o_g = rms_norm(o_g, gla_gain).transpose(0, 2, 1, 3).reshape(B, S, GLA_HEADS * GLA_DV)
    o_gla = (o_g * jax.nn.silu(gg.astype(f32))).astype(xn.dtype)

    rq_h = to_heads(apply_axial_rope(rq.reshape(B, S, RET_HEADS, RET_DK), rope), RET_HEADS)
    rk_h = to_heads(apply_axial_rope(rk.reshape(B, S, RET_HEADS, RET_DK), rope), RET_HEADS) * (RET_DK ** -0.5)
    rv_h = to_heads(rv, RET_HEADS)
    lg_f = jnp.log(1.0 - 2.0 ** (-5.0 - jnp.arange(RET_HEADS, dtype=f32)))
    lg_b = lg_f[::-1]
    o_r = retention_scan(rq_h, rk_h, rv_h, lg_f) + _flip(
        retention_scan(_flip(rq_h), _flip(rk_h), _flip(rv_h), lg_b))
    o_r = head_group_norm(o_r).transpose(0, 2, 1, 3).reshape(B, S, RET_HEADS * RET_DV)
    o_ret = (o_r * jax.nn.silu(rg.astype(f32))).astype(xn.dtype)

    gates = jax.nn.sigmoid((xn @ w_merge).astype(f32)).reshape(B, S, N_BRANCH, D)
    branches = (o_att, o_gla, o_ret)
    merged = gates[:, :, 0] * (branches[0] @ w_branch[0]).astype(f32)
    for i in range(1, N_BRANCH):
        merged = merged + gates[:, :, i] * (branches[i] @ w_branch[i]).astype(f32)
    return merged.astype(xn.dtype) @ w_out


def squared_relu_mlp(x, w_up, w_down):
    h = jax.nn.relu(x @ w_up)
    return (h * h) @ w_down


def trunk(x, norm_mix, norm_mlp, w_in, attn_q_norm, attn_k_norm, gla_w_gate, gla_b_gate,
          gla_out_norm, w_branch, w_merge, w_out, w_up, w_down, norm_final):
    rope = axial_rope_tables(x.shape[1])
    for l in range(DEPTH):
        h = x + mixer(rms_norm(x, norm_mix[l]), w_in[l], attn_q_norm[l], attn_k_norm[l],
                      gla_w_gate[l], gla_b_gate[l], gla_out_norm[l], w_branch[l], w_merge[l], w_out[l], rope)
        x = h + squared_relu_mlp(rms_norm(h, norm_mlp[l]), w_up[l], w_down[l])
    return rms_norm(x, norm_final)


def setup_inputs(seed: int = 0) -> dict:
    key = jax.random.key(seed)
    ks = jax.random.split(key, 20)
    f32 = jnp.float32
    nrm = lambda k, shape, scale: jax.random.normal(k, shape, f32) * scale
    gain = lambda k, shape: 1.0 + 0.02 * jax.random.normal(k, shape, f32)
    return {
        "x_prompt": jax.random.normal(ks[0], (BATCH, SEQ, D_MODEL), f32),
        "x_sample": jax.random.normal(ks[1], (DEC_BATCH, DEC_SEQ, D_MODEL), f32),
        "norm_mix": gain(ks[2], (DEPTH, D_MODEL)),
        "norm_mlp": gain(ks[3], (DEPTH, D_MODEL)),
        "w_in": nrm(ks[4], (DEPTH, D_MODEL, IN_WIDTH), D_MODEL ** -0.5),
        "attn_q_norm": gain(ks[5], (DEPTH, HEAD_DIM)),
        "attn_k_norm": gain(ks[6], (DEPTH, HEAD_DIM)),
        "gla_w_gate": nrm(ks[7], (DEPTH, 2, GLA_GATE_RANK, GLA_HEADS * GLA_DK), GLA_GATE_RANK ** -0.5),
        "gla_b_gate": nrm(ks[8], (DEPTH, 2, GLA_HEADS * GLA_DK), 0.1),
        "gla_out_norm": gain(ks[9], (DEPTH, GLA_DV)),
        "w_branch": nrm(ks[10], (DEPTH, N_BRANCH, BRANCH_W, D_MODEL), BRANCH_W ** -0.5),
        "w_merge": nrm(ks[11], (DEPTH, D_MODEL, N_BRANCH * D_MODEL), D_MODEL ** -0.5),
        "w_out": nrm(ks[12], (DEPTH, D_MODEL, D_MODEL), D_MODEL ** -0.5),
        "w_up": nrm(ks[13], (DEPTH, D_MODEL, D_FF), D_MODEL ** -0.5),
        "w_down": nrm(ks[14], (DEPTH, D_FF, D_MODEL), D_FF ** -0.5),
        "norm_final": gain(ks[15], (D_MODEL,)),
    }


def reference(x_prompt, x_sample, norm_mix, norm_mlp, w_in, attn_q_norm, attn_k_norm, gla_w_gate,
              gla_b_gate, gla_out_norm, w_branch, w_merge, w_out, w_up, w_down, norm_final):
    y_prompt = trunk(x_prompt, norm_mix, norm_mlp, w_in, attn_q_norm, attn_k_norm, gla_w_gate, gla_b_gate,
                     gla_out_norm, w_branch, w_merge, w_out, w_up, w_down, norm_final)
    y_sample = trunk(x_sample, norm_mix, norm_mlp, w_in, attn_q_norm, attn_k_norm, gla_w_gate, gla_b_gate,
                     gla_out_norm, w_branch, w_merge, w_out, w_up, w_down, norm_final)
    return (y_prompt, y_sample)
```

```python
import functools
import math

import jax
import jax.numpy as jnp
from jax import lax
from jax.experimental import pallas as pl
from jax.experimental.pallas import tpu as pltpu

F32 = jnp.float32
BF16 = jnp.bfloat16

D_MODEL = 1024
GRID_W = 64
HEAD_DIM = 64
ATT_HEADS = 8
ATT_KV_HEADS = 2
ROPE_THETA = 10000.0
GLA_HEADS = 4
GLA_DK = 64
GLA_DV = 128
GLA_GATE_RANK = 16
GLA_GATE_TAU = 16.0
GLA_CHUNK = 64
RET_HEADS = 4
RET_DK = 64
RET_DV = 128
RET_CHUNK = 128
BRANCH_W = 512
N_BRANCH = 3
D_FF = 4 * D_MODEL
EPS = 1e-6

LANES = 128
VMEM_LIMIT = 56 * 1024 * 1024

_W_AQ, _W_AK, _W_AV = 512, 128, 128
_W_GQ, _W_GK, _W_GV, _W_GG = 256, 256, 512, 512
_W_RQ, _W_RK, _W_RV, _W_RG = 256, 256, 512, 512
_W_GA = LANES
_OFF = {}
_o = 0
for _n, _w in (("aq", _W_AQ), ("ak", _W_AK), ("av", _W_AV), ("gq", _W_GQ), ("gk", _W_GK), ("gv", _W_GV),
               ("gg", _W_GG), ("rq", _W_RQ), ("rk", _W_RK), ("rv", _W_RV), ("rg", _W_RG), ("ga", _W_GA)):
    _OFF[_n] = (_o, _w)
    _o += _w
IN_WIDTH_PADDED = _o


def _const_spec(shape):
    nd = len(shape)
    return pl.BlockSpec(shape, lambda *_: (0,) * nd, pipeline_mode=pl.Buffered(1))


def _dot(a, b):
    return jnp.dot(a, b, preferred_element_type=F32)


def _dot_nt(a, b):
    return lax.dot_general(a, b, (((1,), (1,)), ((), ())), preferred_element_type=F32)


def _dot_tn(a, b):
    return lax.dot_general(a, b, (((0,), (0,)), ((), ())), preferred_element_type=F32)


def _split3(x):
    hi = x.astype(BF16)
    r = x - hi.astype(F32)
    mid = r.astype(BF16)
    lo = (r - mid.astype(F32)).astype(BF16)
    return hi, mid, lo


def _rms(x, gain):
    return x * lax.rsqrt(jnp.mean(x * x, axis=-1, keepdims=True) + EPS) * gain


def _rope(x, cos, sin_a, sin_b):
    return x * cos + pltpu.roll(x, LANES - 16, 1) * sin_a + pltpu.roll(x, 16, 1) * sin_b


def _proj_kernel(x_ref, gmix_ref, w_ref, wg_ref, bg_ref, qg_ref, kg_ref, cos_ref, sa_ref, sb_ref, seg_ref,
                 aq_ref, ak_ref, av_ref, gq_ref, gk_ref, gv_ref, gg_ref, rq_ref, rk_ref, rv_ref, rg_ref,
                 laf_ref, lab_ref):
    xn = _rms(x_ref[...], gmix_ref[...]).astype(BF16)
    cos, sa, sb = cos_ref[...], sa_ref[...], sb_ref[...]
    seg = seg_ref[...]

    def proj(name):
        off, w = _OFF[name]
        return _dot(xn, w_ref[:, off:off + w])

    def head_norm(t, gain):
        hi, mid, lo = _split3(t * t)
        ss = _dot(hi, seg) + _dot(mid, seg) + _dot(lo, seg)
        return t * lax.rsqrt(ss * (1.0 / HEAD_DIM) + EPS) * gain

    def per_tile(p, out_ref, fn):
        for j in range(p.shape[1] // LANES):
            out_ref[:, j * LANES:(j + 1) * LANES] = fn(p[:, j * LANES:(j + 1) * LANES]).astype(out_ref.dtype)

    scale = HEAD_DIM ** -0.5
    per_tile(proj("aq"), aq_ref, lambda t: _rope(head_norm(t, qg_ref[...]), cos, sa, sb) * scale)
    per_tile(proj("ak"), ak_ref, lambda t: _rope(head_norm(t, kg_ref[...]), cos, sa, sb))
    av_ref[...] = proj("av").astype(BF16)
    gq_ref[...] = (proj("gq") * (GLA_DK ** -0.5)).astype(BF16)
    gk_ref[...] = proj("gk").astype(BF16)
    gv_ref[...] = proj("gv").astype(BF16)
    gg_ref[...] = proj("gg")
    per_tile(proj("rq"), rq_ref, lambda t: _rope(t, cos, sa, sb))
    per_tile(proj("rk"), rk_ref, lambda t: _rope(t, cos, sa, sb) * (RET_DK ** -0.5))
    rv_ref[...] = proj("rv").astype(BF16)
    rg_ref[...] = proj("rg")
    z = proj("ga").astype(BF16)
    pre = _dot(z, wg_ref[...]) + bg_ref[...]
    la = (jnp.minimum(pre, 0.0) - jnp.log1p(jnp.exp(-jnp.abs(pre)))) * (1.0 / GLA_GATE_TAU)
    nk = GLA_HEADS * GLA_DK
    laf_ref[...] = la[:, :nk]
    lab_ref[...] = la[:, nk:]


def _proj(x2d, seq, gmix, w_in, wg, bg, qg, kg, cos, sa, sb, seg):
    t = x2d.shape[0]
    tm = min(512, seq)
    nblk = seq // tm
    row = lambda w, dt: (pl.BlockSpec((tm, w), lambda i: (i, 0)), jax.ShapeDtypeStruct((t, w), dt))
    outs = [row(_W_AQ, BF16), row(_W_AK, BF16), row(_W_AV, BF16), row(_W_GQ, BF16), row(_W_GK, BF16),
            row(_W_GV, BF16), row(_W_GG, F32), row(_W_RQ, BF16), row(_W_RK, BF16), row(_W_RV, BF16),
            row(_W_RG, F32), row(GLA_HEADS * GLA_DK, F32), row(GLA_HEADS * GLA_DK, F32)]
    tab = pl.BlockSpec((tm, LANES), lambda i: (i % nblk, 0))
    return pl.pallas_call(
        _proj_kernel,
        grid=(t // tm,),
        in_specs=[pl.BlockSpec((tm, D_MODEL), lambda i: (i, 0)), _const_spec((1, D_MODEL)),
                  _const_spec(w_in.shape), _const_spec(wg.shape), _const_spec(bg.shape),
                  _const_spec((1, LANES)), _const_spec((1, LANES)), tab, tab, tab, _const_spec((LANES, LANES))],
        out_specs=[o[0] for o in outs],
        out_shape=[o[1] for o in outs],
        compiler_params=pltpu.CompilerParams(dimension_semantics=("parallel",), vmem_limit_bytes=VMEM_LIMIT),
        name="proj",
    )(x2d, gmix, w_in, wg, bg, qg, kg, cos, sa, sb, seg)


def _attn_kernel(q_ref, k_ref, v_ref, o_ref, m_ref, l_ref, acc_ref, *, tq, tk, seq):
    lo_half = lax.broadcasted_iota(jnp.int32, (tq, LANES), 1) < HEAD_DIM
    groups = ATT_HEADS // ATT_KV_HEADS
    qs = []
    for h in range(ATT_HEADS):
        blk = q_ref[:, (h // 2) * LANES:(h // 2 + 1) * LANES].astype(F32)
        src_half, dst_half = h % 2, h // groups
        if src_half != dst_half:
            blk = pltpu.roll(blk, HEAD_DIM, 1)
        keep = lo_half if dst_half == 0 else jnp.logical_not(lo_half)
        qs.append(jnp.where(keep, blk, 0.0).astype(BF16))
    q = jnp.concatenate(qs, axis=0)

    m_ref[...] = jnp.full(m_ref.shape, -jnp.inf, F32)
    l_ref[...] = jnp.zeros(l_ref.shape, F32)
    acc_ref[...] = jnp.zeros(acc_ref.shape, F32)

    def body(c, carry):
        start = pl.multiple_of(c * tk, tk)
        kc = k_ref[pl.ds(start, tk), :]
        vc = v_ref[pl.ds(start, tk), :]
        s = _dot_nt(q, kc)
        m_old = m_ref[...]
        m_new = jnp.maximum(m_old, jnp.max(s, axis=-1, keepdims=True))
        alpha = jnp.exp(m_old - m_new)
        p = jnp.exp(s - m_new)
        l_ref[...] = alpha * l_ref[...] + jnp.sum(p, axis=-1, keepdims=True)
        acc_ref[...] = alpha * acc_ref[...] + _dot(p.astype(BF16), vc)
        m_ref[...] = m_new
        return carry

    lax.fori_loop(0, seq // tk, body, 0)

    out = acc_ref[...] / l_ref[...]
    for j in range(ATT_HEADS // 2):
        a = out[(2 * j) * tq:(2 * j + 1) * tq]
        b = out[(2 * j + 1) * tq:(2 * j + 2) * tq]
        if (2 * j) // groups == 0:
            b = pltpu.roll(b, HEAD_DIM, 1)
        else:
            a = pltpu.roll(a, HEAD_DIM, 1)
        o_ref[:, j * LANES:(j + 1) * LANES] = jnp.where(lo_half, a, b).astype(o_ref.dtype)


def _attention(q, k, v, batch, seq):
    t = q.shape[0]
    tq = 128
    tk = min(512, seq)
    nq = seq // tq
    m_rows = ATT_HEADS * tq
    kv_spec = pl.BlockSpec((seq, LANES), lambda b, i: (b, 0))
    return pl.pallas_call(
        functools.partial(_attn_kernel, tq=tq, tk=tk, seq=seq),
        grid=(batch, nq),
        in_specs=[pl.BlockSpec((tq, _W_AQ), lambda b, i: (b * nq + i, 0)), kv_spec, kv_spec],
        out_specs=pl.BlockSpec((tq, _W_AQ), lambda b, i: (b * nq + i, 0)),
        out_shape=jax.ShapeDtypeStruct((t, _W_AQ), BF16),
        scratch_shapes=[pltpu.VMEM((m_rows, 1), F32), pltpu.VMEM((m_rows, 1), F32),
                        pltpu.VMEM((m_rows, LANES), F32)],
        compiler_params=pltpu.CompilerParams(dimension_semantics=("parallel", "parallel"),
                                             vmem_limit_bytes=VMEM_LIMIT),
        name="attention",
    )(q, k, v)


def _pair_masks(chunk):
    lane_head = lax.broadcasted_iota(jnp.int32, (chunk, LANES), 1) // GLA_DK
    rows = lax.broadcasted_iota(jnp.int32, (2 * GLA_DV, LANES), 0) // GLA_DV
    cols = lax.broadcasted_iota(jnp.int32, (2 * GLA_DV, LANES), 1) // GLA_DK
    return lane_head, rows == cols


def _gla_kernel(*refs, ts, reverse):
    if reverse:
        q_ref, k_ref, v_ref, la_ref, of_ref, gg_ref, gain_ref, o_ref, st_ref = refs
    else:
        q_ref, k_ref, v_ref, la_ref, o_ref, st_ref = refs
    c = GLA_CHUNK
    nchunk = ts // c

    @pl.when(pl.program_id(2) == 0)
    def _():
        st_ref[...] = jnp.zeros(st_ref.shape, F32)

    lane_head, diag = _pair_masks(c)
    ri = lax.broadcasted_iota(jnp.int32, (c, c), 0)
    ci = lax.broadcasted_iota(jnp.int32, (c, c), 1)
    causal = (ri <= ci) if reverse else (ri >= ci)
    cum = causal.astype(BF16)
    mid = c // 2 if reverse else c // 2 - 1
    last = 0 if reverse else c - 1

    def body(n, carry):
        ch = (nchunk - 1 - n) if reverse else n
        rows = pl.ds(pl.multiple_of(ch * c, c), c)
        qf = q_ref[rows, :].astype(F32)
        kf = k_ref[rows, :].astype(F32)
        vc = v_ref[rows, :]
        hi, md, lo = _split3(la_ref[rows, :])
        b = _dot(cum, hi) + _dot(cum, md) + _dot(cum, lo)
        b_ref = b[mid:mid + 1, :]
        b_last = b[last:last + 1, :]
        q_in = qf * jnp.exp(b - b_ref)
        k_in = (kf * jnp.exp(b_ref - b)).astype(BF16)
        st = st_ref[...]
        inter = _dot_nt((qf * jnp.exp(b)).astype(BF16), st.astype(BF16))
        intra = []
        for hh in range(2):
            qm = jnp.where(lane_head == hh, q_in, 0.0).astype(BF16)
            s = jnp.where(causal, _dot_nt(qm, k_in), 0.0)
            intra.append(_dot(s.astype(BF16), vc[:, hh * GLA_DV:(hh + 1) * GLA_DV]))
        o = jnp.concatenate(intra, axis=1) + inter
        kd = (kf * jnp.exp(b_last - b)).astype(BF16)
        upd = jnp.where(diag, _dot_tn(vc, kd), 0.0)
        st_ref[...] = st * jnp.exp(b_last) + upd
        if reverse:
            o = o + of_ref[rows, :]
            gate = gg_ref[rows, :]
            gate = gate * jax.nn.sigmoid(gate)
            for hh in range(2):
                sl = slice(hh * GLA_DV, (hh + 1) * GLA_DV)
                o_ref[rows, sl] = (_rms(o[:, sl], gain_ref[...]) * gate[:, sl]).astype(o_ref.dtype)
        else:
            o_ref[rows, :] = o
        return carry

    lax.fori_loop(0, nchunk, body, 0)


def _scan_specs(batch, seq, ts, reverse):
    ns = seq // ts
    blk = (lambda i: ns - 1 - i) if reverse else (lambda i: i)
    narrow = pl.BlockSpec((ts, LANES), lambda b, p, i: (b * ns + blk(i), p))
    wide = pl.BlockSpec((ts, 2 * GLA_DV), lambda b, p, i: (b * ns + blk(i), p))
    return ns, narrow, wide


def _gla(q, k, v, la_f, la_b, gg, gain, batch, seq):
    t = q.shape[0]
    ts = min(1024, seq)
    params = pltpu.CompilerParams(dimension_semantics=("parallel", "parallel", "arbitrary"),
                                  vmem_limit_bytes=VMEM_LIMIT)
    scratch = [pltpu.VMEM((2 * GLA_DV, LANES), F32)]
    ns, narrow, wide = _scan_specs(batch, seq, ts, False)
    o_f = pl.pallas_call(
        functools.partial(_gla_kernel, ts=ts, reverse=False),
        grid=(batch, GLA_HEADS // 2, ns),
        in_specs=[narrow, narrow, wide, narrow],
        out_specs=wide,
        out_shape=jax.ShapeDtypeStruct((t, GLA_HEADS * GLA_DV), F32),
        scratch_shapes=scratch, compiler_params=params, name="gla_fwd",
    )(q, k, v, la_f)
    ns, narrow, wide = _scan_specs(batch, seq, ts, True)
    return pl.pallas_call(
        functools.partial(_gla_kernel, ts=ts, reverse=True),
        grid=(batch, GLA_HEADS // 2, ns),
        in_specs=[narrow, narrow, wide, narrow, wide, wide, _const_spec((1, GLA_DV))],
        out_specs=wide,
        out_shape=jax.ShapeDtypeStruct((t, GLA_HEADS * GLA_DV), BF16),
        scratch_shapes=scratch, compiler_params=params, name="gla_bwd",
    )(q, k, v, la_b, o_f, gg, gain)


def _ret_kernel(*refs, ts, reverse):
    if reverse:
        q_ref, k_ref, v_ref, dm_ref, qd_ref, kd_ref, cd_ref, of_ref, rg_ref, o_ref, st_ref = refs
    else:
        q_ref, k_ref, v_ref, dm_ref, qd_ref, kd_ref, cd_ref, o_ref, st_ref = refs
    c = RET_CHUNK
    nchunk = ts // c

    @pl.when(pl.program_id(2) == 0)
    def _():
        st_ref[...] = jnp.zeros(st_ref.shape, F32)

    lane_head, diag = _pair_masks(c)

    def body(n, carry):
        ch = (nchunk - 1 - n) if reverse else n
        rows = pl.ds(pl.multiple_of(ch * c, c), c)
        qc = q_ref[rows, :]
        kc = k_ref[rows, :]
        vc = v_ref[rows, :]
        st = st_ref[...]
        inter = _dot_nt(qc, st.astype(BF16)) * qd_ref[0]
        intra = []
        for hh in range(2):
            qm = jnp.where(lane_head == hh, qc, jnp.zeros_like(qc))
            s = _dot_nt(qm, kc) * dm_ref[0, hh]
            intra.append(_dot(s.astype(BF16), vc[:, hh * RET_DV:(hh + 1) * RET_DV]))
        o = jnp.concatenate(intra, axis=1) + inter
        kd = (kc.astype(F32) * kd_ref[0]).astype(BF16)
        upd = jnp.where(diag, _dot_tn(vc, kd), 0.0)
        st_ref[...] = st * cd_ref[0] + upd
        if reverse:
            o = o + of_ref[rows, :]
            gate = rg_ref[rows, :]
            gate = gate * jax.nn.sigmoid(gate)
            for hh in range(2):
                sl = slice(hh * RET_DV, (hh + 1) * RET_DV)
                xc = o[:, sl] - jnp.mean(o[:, sl], axis=-1, keepdims=True)
                y = xc * lax.rsqrt(jnp.mean(xc * xc, axis=-1, keepdims=True) + EPS)
                o_ref[rows, sl] = (y * gate[:, sl]).astype(o_ref.dtype)
        else:
            o_ref[rows, :] = o
        return carry

    lax.fori_loop(0, nchunk, body, 0)


def _ret_tables(reverse):
    c = RET_CHUNK
    lg = jnp.log(1.0 - 2.0 ** (-5.0 - jnp.arange(RET_HEADS, dtype=F32)))
    idx = jnp.arange(c, dtype=F32)
    if reverse:
        lg = lg[::-1]
        diff = idx[None, :] - idx[:, None]
        q_pow, k_pow = c - idx, idx
    else:
        diff = idx[:, None] - idx[None, :]
        q_pow, k_pow = idx + 1.0, c - 1.0 - idx
    lgh = lg[:, None, None]
    dm = jnp.where(diff >= 0, jnp.exp(lgh * jnp.maximum(diff, 0.0)), 0.0)
    qd = jnp.exp(lg[:, None] * q_pow[None, :])
    kd = jnp.exp(lg[:, None] * k_pow[None, :])
    cd = jnp.exp(lg * c)
    npair = RET_HEADS // 2
    dm = dm.reshape(npair, 2, c, c)
    qd = jnp.repeat(qd.reshape(npair, 2, c).transpose(0, 2, 1), RET_DV, axis=2)
    kd = jnp.repeat(kd.reshape(npair, 2, c).transpose(0, 2, 1), RET_DK, axis=2)
    cd = jnp.repeat(cd.reshape(npair, 1, 2), RET_DK, axis=2)
    return dm, qd, kd, cd


def _retention(q, k, v, rg, batch, seq):
    t = q.shape[0]
    ts = min(1024, seq)
    c = RET_CHUNK
    params = pltpu.CompilerParams(dimension_semantics=("parallel", "parallel", "arbitrary"),
                                  vmem_limit_bytes=VMEM_LIMIT)
    scratch = [pltpu.VMEM((2 * RET_DV, LANES), F32)]
    tab_specs = [pl.BlockSpec((1, 2, c, c), lambda b, p, i: (p, 0, 0, 0)),
                 pl.BlockSpec((1, c, 2 * RET_DV), lambda b, p, i: (p, 0, 0)),
                 pl.BlockSpec((1, c, LANES), lambda b, p, i: (p, 0, 0)),
                 pl.BlockSpec((1, 1, LANES), lambda b, p, i: (p, 0, 0))]
    ns, narrow, wide = _scan_specs(batch, seq, ts, False)
    o_f = pl.pallas_call(
        functools.partial(_ret_kernel, ts=ts, reverse=False),
        grid=(batch, RET_HEADS // 2, ns),
        in_specs=[narrow, narrow, wide] + tab_specs,
        out_specs=wide,
        out_shape=jax.ShapeDtypeStruct((t, RET_HEADS * RET_DV), F32),
        scratch_shapes=scratch, compiler_params=params, name="ret_fwd",
    )(q, k, v, *_ret_tables(False))
    ns, narrow, wide = _scan_specs(batch, seq, ts, True)
    return pl.pallas_call(
        functools.partial(_ret_kernel, ts=ts, reverse=True),
        grid=(batch, RET_HEADS // 2, ns),
        in_specs=[narrow, narrow, wide] + tab_specs + [wide, wide],
        out_specs=wide,
        out_shape=jax.ShapeDtypeStruct((t, RET_HEADS * RET_DV), BF16),
        scratch_shapes=scratch, compiler_params=params, name="ret_bwd",
    )(q, k, v, *_ret_tables(True), o_f, rg)


def _merge_mlp_kernel(x_ref, oa_ref, og_ref, or_ref, gmix_ref, gmlp_ref, gfin_ref, wm_ref, wb_ref, wo_ref,
                      wu_ref, wd_ref, y_ref, *, final):
    x = x_ref[...]
    xn = _rms(x, gmix_ref[...]).astype(BF16)
    merged = None
    for i, b_ref in enumerate((oa_ref, og_ref, or_ref)):
        gate = jax.nn.sigmoid(_dot(xn, wm_ref[:, i * D_MODEL:(i + 1) * D_MODEL]))
        term = gate * _dot(b_ref[...], wb_ref[i])
        merged = term if merged is None else merged + term
    h = x + _dot(merged.astype(BF16), wo_ref[...])
    hn = _rms(h, gmlp_ref[...]).astype(BF16)
    y = h
    for j in range(D_FF // D_MODEL):
        u = jnp.maximum(_dot(hn, wu_ref[:, j * D_MODEL:(j + 1) * D_MODEL]), 0.0)
        y = y + _dot((u * u).astype(BF16), wd_ref[j * D_MODEL:(j + 1) * D_MODEL, :])
    if final:
        y = _rms(y, gfin_ref[...])
    y_ref[...] = y


def _merge_mlp(x2d, oa, og, o_r, gmix, gmlp, gfin, wm, wb, wo, wu, wd, final):
    t = x2d.shape[0]
    tm = min(512, t)
    row = lambda w: pl.BlockSpec((tm, w), lambda i: (i, 0))
    vec = _const_spec((1, D_MODEL))
    return pl.pallas_call(
        functools.partial(_merge_mlp_kernel, final=final),
        grid=(t // tm,),
        in_specs=[row(D_MODEL), row(BRANCH_W), row(BRANCH_W), row(BRANCH_W), vec, vec, vec,
                  _const_spec(wm.shape), _const_spec(wb.shape), _const_spec(wo.shape),
                  _const_spec(wu.shape), _const_spec(wd.shape)],
        out_specs=row(D_MODEL),
        out_shape=jax.ShapeDtypeStruct((t, D_MODEL), F32),
        compiler_params=pltpu.CompilerParams(dimension_semantics=("parallel",), vmem_limit_bytes=VMEM_LIMIT),
        name="merge_mlp",
    )(x2d, oa, og, o_r, gmix, gmlp, gfin, wm, wb, wo, wu, wd)


def _rope_tables(seq):
    pos = jnp.arange(seq)
    axis_dim = HEAD_DIM // 2
    inv = ROPE_THETA ** (-jnp.arange(0, axis_dim, 2, dtype=F32) / axis_dim)
    ang_r = (pos // GRID_W).astype(F32)[:, None] * inv[None, :]
    ang_c = (pos % GRID_W).astype(F32)[:, None] * inv[None, :]
    ang = jnp.concatenate([ang_r, ang_r, ang_c, ang_c], axis=-1)
    ang = jnp.concatenate([ang, ang], axis=-1)
    first = (jnp.arange(LANES) % axis_dim) < axis_dim // 2
    cos, sin = jnp.cos(ang), jnp.sin(ang)
    return cos, jnp.where(first, -sin, 0.0), jnp.where(first, 0.0, sin)


def _prep_layer(w_in, gla_w_gate, gla_b_gate, q_gain, k_gain, gla_gain, w_branch, w_merge, w_out, w_up, w_down):
    splits = (512, 128, 128, 256, 256, 512, 32, 512, 256, 256, 512, 512)
    names = ("aq", "ak", "av", "gq", "gk", "gv", "ga", "gg", "rq", "rk", "rv", "rg")
    cols, start = {}, 0
    for n, w in zip(names, splits):
        cols[n] = w_in[:, start:start + w]
        start += w
    cols["ga"] = jnp.pad(cols["ga"], ((0, 0), (0, _W_GA - 2 * GLA_GATE_RANK)))
    order = sorted(_OFF, key=lambda n: _OFF[n][0])
    w_in_r = jnp.concatenate([cols[n] for n in order], axis=1).astype(BF16)
    nk = GLA_HEADS * GLA_DK
    r = GLA_GATE_RANK
    wg = jnp.zeros((_W_GA, 2 * nk), F32)
    wg = wg.at[:r, :nk].set(gla_w_gate[0]).at[r:2 * r, nk:].set(gla_w_gate[1]).astype(BF16)
    bg = jnp.concatenate([gla_b_gate[0], gla_b_gate[1]])[None, :]
    two = lambda g: jnp.concatenate([g, g])[None, :]
    return dict(w_in=w_in_r, wg=wg, bg=bg, qg=two(q_gain), kg=two(k_gain), gla_gain=gla_gain[None, :],
                wb=w_branch.astype(BF16), wm=w_merge.astype(BF16), wo=w_out.astype(BF16),
                wu=w_up.astype(BF16), wd=w_down.astype(BF16))


def _trunk(x, layers, norm_mix, norm_mlp, norm_final):
    batch, seq, _ = x.shape
    x2d = x.reshape(batch * seq, D_MODEL)
    cos, sa, sb = _rope_tables(seq)
    seg = (jnp.arange(LANES)[:, None] // HEAD_DIM == jnp.arange(LANES)[None, :] // HEAD_DIM).astype(BF16)
    depth = len(layers)
    for l, w in enumerate(layers):
        gmix = norm_mix[l][None, :]
        (aq, ak, av, gq, gk, gv, gg, rq, rk, rv, rg, la_f, la_b) = _proj(
            x2d, seq, gmix, w["w_in"], w["wg"], w["bg"], w["qg"], w["kg"], cos, sa, sb, seg)
        o_att = _attention(aq, ak, av, batch, seq)
        o_gla = _gla(gq, gk, gv, la_f, la_b, gg, w["gla_gain"], batch, seq)
        o_ret = _retention(rq, rk, rv, rg, batch, seq)
        x2d = _merge_mlp(x2d, o_att, o_gla, o_ret, gmix, norm_mlp[l][None, :], norm_final[None, :],
                         w["wm"], w["wb"], w["wo"], w["wu"], w["wd"], final=(l == depth - 1))
    return x2d.reshape(batch, seq, D_MODEL)


def kernel(x_prompt, x_sample, norm_mix, norm_mlp, w_in, attn_q_norm, attn_k_norm, gla_w_gate, gla_b_gate,
           gla_out_norm, w_branch, w_merge, w_out, w_up, w_down, norm_final):
    layers = [_prep_layer(w_in[l], gla_w_gate[l], gla_b_gate[l], attn_q_norm[l], attn_k_norm[l],
                          gla_out_norm[l], w_branch[l], w_merge[l], w_out[l], w_up[l], w_down[l])
              for l in range(w_in.shape[0])]
    y_prompt = _trunk(x_prompt, layers, norm_mix, norm_mlp, norm_final)
    y_sample = _trunk(x_sample, layers, norm_mix, norm_mlp, norm_final)
    return (y_prompt, y_sample)
```

```python
import functools
import math

import jax
import jax.numpy as jnp
from jax import lax
from jax.experimental import pallas as pl
from jax.experimental.pallas import tpu as pltpu

F32 = jnp.float32
BF16 = jnp.bfloat16

D_MODEL = 1024
GRID_W = 64
HEAD_DIM = 64
ATT_HEADS = 8
ATT_KV_HEADS = 2
ROPE_THETA = 10000.0
GLA_HEADS = 4
GLA_DK = 64
GLA_DV = 128
GLA_GATE_RANK = 16
GLA_GATE_TAU = 16.0
GLA_CHUNK = 64
RET_HEADS = 4
RET_DK = 64
RET_DV = 128
RET_CHUNK = 128
BRANCH_W = 512
N_BRANCH = 3
D_FF = 4 * D_MODEL
EPS = 1e-6

ATT_GROUPS = ATT_HEADS // ATT_KV_HEADS
ATT_Q_SCALE = HEAD_DIM ** -0.5 * math.log2(math.e)
ATT_MAX_SHIFT = 40.0
SCAN_UNROLL = 8

LANES = 128
VMEM_LIMIT = 56 * 1024 * 1024

_W_AQ, _W_AK, _W_AV = 512, 128, 128
_W_GQ, _W_GK, _W_GV, _W_GG = 256, 256, 512, 512
_W_RQ, _W_RK, _W_RV, _W_RG = 256, 256, 512, 512
_W_GA = LANES
_OFF = {}
_o = 0
for _n, _w in (("aq", _W_AQ), ("ak", _W_AK), ("av", _W_AV), ("gq", _W_GQ), ("gk", _W_GK), ("gv", _W_GV),
               ("gg", _W_GG), ("rq", _W_RQ), ("rk", _W_RK), ("rv", _W_RV), ("rg", _W_RG), ("ga", _W_GA)):
    _OFF[_n] = (_o, _w)
    _o += _w
IN_WIDTH_PADDED = _o


def _const_spec(shape):
    nd = len(shape)
    return pl.BlockSpec(shape, lambda *_: (0,) * nd, pipeline_mode=pl.Buffered(1))


def _dot(a, b):
    return jnp.dot(a, b, preferred_element_type=F32)


def _dot_nt(a, b):
    return lax.dot_general(a, b, (((1,), (1,)), ((), ())), preferred_element_type=F32)


def _dot_tn(a, b):
    return lax.dot_general(a, b, (((0,), (0,)), ((), ())), preferred_element_type=F32)


def _split3(x):
    hi = x.astype(BF16)
    r = x - hi.astype(F32)
    mid = r.astype(BF16)
    lo = (r - mid.astype(F32)).astype(BF16)
    return hi, mid, lo


def _rms(x, gain):
    return x * lax.rsqrt(jnp.mean(x * x, axis=-1, keepdims=True) + EPS) * gain


def _rope(x, cos, sin_a, sin_b):
    return x * cos + pltpu.roll(x, LANES - 16, 1) * sin_a + pltpu.roll(x, 16, 1) * sin_b


def _proj_kernel(x_ref, gmix_ref, w_ref, wg_ref, bg_ref, qg_ref, kg_ref, cos_ref, sa_ref, sb_ref, seg_ref,
                 aq_ref, ak_ref, av_ref, gq_ref, gk_ref, gv_ref, gg_ref, rq_ref, rk_ref, rv_ref, rg_ref,
                 laf_ref, lab_ref):
    xn = _rms(x_ref[...], gmix_ref[...]).astype(BF16)
    cos, sa, sb = cos_ref[...], sa_ref[...], sb_ref[...]
    seg = seg_ref[...]

    def proj(name):
        off, w = _OFF[name]
        return _dot(xn, w_ref[:, off:off + w])

    def head_norm(t, gain):
        hi, mid, lo = _split3(t * t)
        ss = _dot(hi, seg) + _dot(mid, seg) + _dot(lo, seg)
        return t * lax.rsqrt(ss * (1.0 / HEAD_DIM) + EPS) * gain

    def per_tile(p, out_ref, fn):
        for j in range(p.shape[1] // LANES):
            out_ref[:, j * LANES:(j + 1) * LANES] = fn(p[:, j * LANES:(j + 1) * LANES]).astype(out_ref.dtype)

    per_tile(proj("aq"), aq_ref, lambda t: _rope(head_norm(t, qg_ref[...]), cos, sa, sb) * ATT_Q_SCALE)
    per_tile(proj("ak"), ak_ref, lambda t: _rope(head_norm(t, kg_ref[...]), cos, sa, sb))
    av_ref[:, :_W_AV] = proj("av").astype(BF16)
    av_ref[:, _W_AV:] = jnp.ones((av_ref.shape[0], LANES), BF16)
    gq_ref[...] = (proj("gq") * (GLA_DK ** -0.5)).astype(BF16)
    gk_ref[...] = proj("gk").astype(BF16)
    gv_ref[...] = proj("gv").astype(BF16)
    gg_ref[...] = proj("gg")
    per_tile(proj("rq"), rq_ref, lambda t: _rope(t, cos, sa, sb))
    per_tile(proj("rk"), rk_ref, lambda t: _rope(t, cos, sa, sb) * (RET_DK ** -0.5))
    rv_ref[...] = proj("rv").astype(BF16)
    rg_ref[...] = proj("rg")
    z = proj("ga").astype(BF16)
    pre = _dot(z, wg_ref[...]) + bg_ref[...]
    la = (jnp.minimum(pre, 0.0) - jnp.log1p(jnp.exp(-jnp.abs(pre)))) * (1.0 / GLA_GATE_TAU)
    nk = GLA_HEADS * GLA_DK
    laf_ref[...] = la[:, :nk]
    lab_ref[...] = la[:, nk:]


def _proj(x2d, seq, gmix, w_in, wg, bg, qg, kg, cos, sa, sb, seg):
    t = x2d.shape[0]
    tm = min(512, seq)
    nblk = seq // tm
    row = lambda w, dt: (pl.BlockSpec((tm, w), lambda i: (i, 0)), jax.ShapeDtypeStruct((t, w), dt))
    outs = [row(_W_AQ, BF16), row(_W_AK, BF16), row(_W_AV + LANES, BF16), row(_W_GQ, BF16), row(_W_GK, BF16),
            row(_W_GV, BF16), row(_W_GG, F32), row(_W_RQ, BF16), row(_W_RK, BF16), row(_W_RV, BF16),
            row(_W_RG, F32), row(GLA_HEADS * GLA_DK, F32), row(GLA_HEADS * GLA_DK, F32)]
    tab = pl.BlockSpec((tm, LANES), lambda i: (i % nblk, 0))
    return pl.pallas_call(
        _proj_kernel,
        grid=(t // tm,),
        in_specs=[pl.BlockSpec((tm, D_MODEL), lambda i: (i, 0)), _const_spec((1, D_MODEL)),
                  _const_spec(w_in.shape), _const_spec(wg.shape), _const_spec(bg.shape),
                  _const_spec((1, LANES)), _const_spec((1, LANES)), tab, tab, tab, _const_spec((LANES, LANES))],
        out_specs=[o[0] for o in outs],
        out_shape=[o[1] for o in outs],
        compiler_params=pltpu.CompilerParams(dimension_semantics=("parallel",), vmem_limit_bytes=VMEM_LIMIT),
        name="proj",
    )(x2d, gmix, w_in, wg, bg, qg, kg, cos, sa, sb, seg)


def _stack_query_heads(q_ref, tq):
    lo_half = lax.broadcasted_iota(jnp.int32, (tq, LANES), 1) < HEAD_DIM
    qs = []
    for h in range(ATT_HEADS):
        blk = q_ref[:, (h // 2) * LANES:(h // 2 + 1) * LANES].astype(F32)
        src_half, dst_half = h % 2, h // ATT_GROUPS
        if src_half != dst_half:
            blk = pltpu.roll(blk, HEAD_DIM, 1)
        keep = lo_half if dst_half == 0 else jnp.logical_not(lo_half)
        qs.append(jnp.where(keep, blk, 0.0).astype(BF16))
    return jnp.concatenate(qs, axis=0)


def _store_heads(acc_ref, o_ref, tq):
    lo_half = lax.broadcasted_iota(jnp.int32, (tq, LANES), 1) < HEAD_DIM
    out = acc_ref[:, :LANES] / acc_ref[:, LANES:]
    for j in range(ATT_HEADS // 2):
        a = out[(2 * j) * tq:(2 * j + 1) * tq]
        b = out[(2 * j + 1) * tq:(2 * j + 2) * tq]
        if (2 * j) // ATT_GROUPS == 0:
            b = pltpu.roll(b, HEAD_DIM, 1)
        else:
            a = pltpu.roll(a, HEAD_DIM, 1)
        o_ref[:, j * LANES:(j + 1) * LANES] = jnp.where(lo_half, a, b).astype(o_ref.dtype)


def _attn_shifted_kernel(shift_ref, q_ref, k_ref, v_ref, o_ref, acc_ref, *, tq, tk, seq):
    q = _stack_query_heads(q_ref, tq)
    shift = shift_ref[0]
    acc_ref[...] = jnp.zeros(acc_ref.shape, F32)

    def body(c, carry):
        start = pl.multiple_of(c * tk, tk)
        p = jnp.exp2(_dot_nt(q, k_ref[pl.ds(start, tk), :]) - shift).astype(BF16)
        acc_ref[...] += _dot(p, v_ref[pl.ds(start, tk), :])
        return carry

    lax.fori_loop(0, seq // tk, body, 0, unroll=2)
    _store_heads(acc_ref, o_ref, tq)


def _attn_online_kernel(q_ref, k_ref, v_ref, o_ref, acc_ref, m_ref, *, tq, tk, seq):
    q = _stack_query_heads(q_ref, tq)
    m_ref[...] = jnp.full(m_ref.shape, -jnp.inf, F32)
    acc_ref[...] = jnp.zeros(acc_ref.shape, F32)

    def body(c, carry):
        start = pl.multiple_of(c * tk, tk)
        s = _dot_nt(q, k_ref[pl.ds(start, tk), :])
        m_old = m_ref[...]
        m_new = jnp.maximum(m_old, jnp.max(s, axis=-1, keepdims=True))
        p = jnp.exp2(s - m_new).astype(BF16)
        acc_ref[...] = jnp.exp2(m_old - m_new) * acc_ref[...] + _dot(p, v_ref[pl.ds(start, tk), :])
        m_ref[...] = m_new
        return carry

    lax.fori_loop(0, seq // tk, body, 0)
    _store_heads(acc_ref, o_ref, tq)


def _attention(q, k, v, shift, batch, seq):
    t = q.shape[0]
    tq = 128
    tk = min(512, seq)
    nq = seq // tq
    m_rows = ATT_HEADS * tq
    row_spec = pl.BlockSpec((tq, _W_AQ), lambda b, i: (b * nq + i, 0))
    k_spec = pl.BlockSpec((seq, LANES), lambda b, i: (b, 0))
    v_spec = pl.BlockSpec((seq, 2 * LANES), lambda b, i: (b, 0))
    common = dict(
        grid=(batch, nq),
        out_specs=row_spec,
        out_shape=jax.ShapeDtypeStruct((t, _W_AQ), BF16),
        compiler_params=pltpu.CompilerParams(dimension_semantics=("parallel", "parallel"),
                                             vmem_limit_bytes=VMEM_LIMIT))
    acc = pltpu.VMEM((m_rows, 2 * LANES), F32)

    def shifted(q, k, v):
        return pl.pallas_call(
            functools.partial(_attn_shifted_kernel, tq=tq, tk=tk, seq=seq),
            in_specs=[pl.BlockSpec(memory_space=pltpu.SMEM), row_spec, k_spec, v_spec],
            scratch_shapes=[acc], name="attention", **common)(shift, q, k, v)

    def online(q, k, v):
        return pl.pallas_call(
            functools.partial(_attn_online_kernel, tq=tq, tk=tk, seq=seq),
            in_specs=[row_spec, k_spec, v_spec],
            scratch_shapes=[acc, pltpu.VMEM((m_rows, 1), F32)], name="attention_online", **common)(q, k, v)

    return lax.cond(shift[0] <= ATT_MAX_SHIFT, shifted, online, q, k, v)


def _pair_masks(chunk):
    lane_head = lax.broadcasted_iota(jnp.int32, (chunk, LANES), 1) // GLA_DK
    rows = lax.broadcasted_iota(jnp.int32, (2 * GLA_DV, LANES), 0) // GLA_DV
    cols = lax.broadcasted_iota(jnp.int32, (2 * GLA_DV, LANES), 1) // GLA_DK
    return lane_head, rows == cols


def _gla_kernel(*refs, ts, reverse):
    if reverse:
        q_ref, k_ref, v_ref, la_ref, of_ref, gg_ref, gain_ref, o_ref, st_ref = refs
    else:
        q_ref, k_ref, v_ref, la_ref, o_ref, st_ref = refs
    c = GLA_CHUNK
    nchunk = ts // c

    @pl.when(pl.program_id(2) == 0)
    def _():
        st_ref[...] = jnp.zeros(st_ref.shape, F32)

    lane_head, diag = _pair_masks(c)
    ri = lax.broadcasted_iota(jnp.int32, (c, c), 0)
    ci = lax.broadcasted_iota(jnp.int32, (c, c), 1)
    causal = (ri <= ci) if reverse else (ri >= ci)
    cum = causal.astype(BF16)
    mid = c // 2 if reverse else c // 2 - 1
    last = 0 if reverse else c - 1

    def local_parts(rows_list):
        cums = [_dot(cum, jnp.concatenate(_split3(la_ref[rows, :]), axis=1)) for rows in rows_list]
        scores, tails = [], []
        for rows, parts in zip(rows_list, cums):
            qf = q_ref[rows, :].astype(F32)
            kf = k_ref[rows, :].astype(F32)
            b = parts[:, :LANES] + parts[:, LANES:2 * LANES] + parts[:, 2 * LANES:]
            b_ref = b[mid:mid + 1, :]
            b_last = b[last:last + 1, :]
            q_in = qf * jnp.exp(b - b_ref)
            k_in = (kf * jnp.exp(b_ref - b)).astype(BF16)
            scores.append([_dot_nt(jnp.where(lane_head == hh, q_in, 0.0).astype(BF16), k_in) for hh in range(2)])
            kd = (kf * jnp.exp(b_last - b)).astype(BF16)
            tails.append(((qf * jnp.exp(b)).astype(BF16), jnp.exp(b_last), kd))
        out = []
        for rows, ss, (q_dec, decay, kd) in zip(rows_list, scores, tails):
            vc = v_ref[rows, :]
            intra = [_dot(jnp.where(causal, s, 0.0).astype(BF16), vc[:, hh * GLA_DV:(hh + 1) * GLA_DV])
                     for hh, s in enumerate(ss)]
            upd = jnp.where(diag, _dot_tn(vc, kd), 0.0)
            out.append((jnp.concatenate(intra, axis=1), q_dec, decay, upd))
        return out

    def finish(rows, o):
        if not reverse:
            o_ref[rows, :] = o
            return
        o = o + of_ref[rows, :]
        gate = gg_ref[rows, :]
        gate = gate * jax.nn.sigmoid(gate)
        for hh in range(2):
            sl = slice(hh * GLA_DV, (hh + 1) * GLA_DV)
            o_ref[rows, sl] = (_rms(o[:, sl], gain_ref[...]) * gate[:, sl]).astype(o_ref.dtype)

    unroll = min(SCAN_UNROLL, nchunk)

    def body(n, carry):
        rows = []
        for u in range(unroll):
            idx = n * unroll + u
            ch = (nchunk - 1 - idx) if reverse else idx
            rows.append(pl.ds(pl.multiple_of(ch * c, c), c))
        local = local_parts(rows)
        st = st_ref[...]
        for r, (intra, q_dec, decay, upd) in zip(rows, local):
            finish(r, intra + _dot_nt(q_dec, st.astype(BF16)))
            st = st * decay + upd
        st_ref[...] = st
        return carry

    lax.fori_loop(0, nchunk // unroll, body, 0)


def _scan_specs(batch, seq, ts, reverse):
    ns = seq // ts
    blk = (lambda i: ns - 1 - i) if reverse else (lambda i: i)
    narrow = pl.BlockSpec((ts, LANES), lambda b, p, i: (b * ns + blk(i), p))
    wide = pl.BlockSpec((ts, 2 * GLA_DV), lambda b, p, i: (b * ns + blk(i), p))
    return ns, narrow, wide


def _gla(q, k, v, la_f, la_b, gg, gain, batch, seq):
    t = q.shape[0]
    ts = min(1024, seq)
    params = pltpu.CompilerParams(dimension_semantics=("parallel", "parallel", "arbitrary"),
                                  vmem_limit_bytes=VMEM_LIMIT)
    scratch = [pltpu.VMEM((2 * GLA_DV, LANES), F32)]
    ns, narrow, wide = _scan_specs(batch, seq, ts, False)
    o_f = pl.pallas_call(
        functools.partial(_gla_kernel, ts=ts, reverse=False),
        grid=(batch, GLA_HEADS // 2, ns),
        in_specs=[narrow, narrow, wide, narrow],
        out_specs=wide,
        out_shape=jax.ShapeDtypeStruct((t, GLA_HEADS * GLA_DV), F32),
        scratch_shapes=scratch, compiler_params=params, name="gla_fwd",
    )(q, k, v, la_f)
    ns, narrow, wide = _scan_specs(batch, seq, ts, True)
    return pl.pallas_call(
        functools.partial(_gla_kernel, ts=ts, reverse=True),
        grid=(batch, GLA_HEADS // 2, ns),
        in_specs=[narrow, narrow, wide, narrow, wide, wide, _const_spec((1, GLA_DV))],
        out_specs=wide,
        out_shape=jax.ShapeDtypeStruct((t, GLA_HEADS * GLA_DV), BF16),
        scratch_shapes=scratch, compiler_params=params, name="gla_bwd",
    )(q, k, v, la_b, o_f, gg, gain)


def _ret_kernel(*refs, ts, reverse):
    if reverse:
        q_ref, k_ref, v_ref, dm_ref, qd_ref, kd_ref, cd_ref, of_ref, rg_ref, o_ref, st_ref = refs
    else:
        q_ref, k_ref, v_ref, dm_ref, qd_ref, kd_ref, cd_ref, o_ref, st_ref = refs
    c = RET_CHUNK
    nchunk = ts // c

    @pl.when(pl.program_id(2) == 0)
    def _():
        st_ref[...] = jnp.zeros(st_ref.shape, F32)

    lane_head, diag = _pair_masks(c)

    def local_parts(rows_list):
        scores = []
        for rows in rows_list:
            qc = q_ref[rows, :]
            kc = k_ref[rows, :]
            scores.append([_dot_nt(jnp.where(lane_head == hh, qc, jnp.zeros_like(qc)), kc) for hh in range(2)])
        out = []
        for rows, ss in zip(rows_list, scores):
            vc = v_ref[rows, :]
            intra = [_dot((s * dm_ref[0, hh]).astype(BF16), vc[:, hh * RET_DV:(hh + 1) * RET_DV])
                     for hh, s in enumerate(ss)]
            kd = (k_ref[rows, :].astype(F32) * kd_ref[0]).astype(BF16)
            out.append((jnp.concatenate(intra, axis=1), q_ref[rows, :], jnp.where(diag, _dot_tn(vc, kd), 0.0)))
        return out

    def finish(rows, o):
        if not reverse:
            o_ref[rows, :] = o
            return
        o = o + of_ref[rows, :]
        gate = rg_ref[rows, :]
        gate = gate * jax.nn.sigmoid(gate)
        for hh in range(2):
            sl = slice(hh * RET_DV, (hh + 1) * RET_DV)
            xc = o[:, sl] - jnp.mean(o[:, sl], axis=-1, keepdims=True)
            y = xc * lax.rsqrt(jnp.mean(xc * xc, axis=-1, keepdims=True) + EPS)
            o_ref[rows, sl] = (y * gate[:, sl]).astype(o_ref.dtype)

    unroll = min(SCAN_UNROLL, nchunk)

    def body(n, carry):
        rows = []
        for u in range(unroll):
            idx = n * unroll + u
            ch = (nchunk - 1 - idx) if reverse else idx
            rows.append(pl.ds(pl.multiple_of(ch * c, c), c))
        local = local_parts(rows)
        st = st_ref[...]
        for r, (intra, qc, upd) in zip(rows, local):
            finish(r, intra + _dot_nt(qc, st.astype(BF16)) * qd_ref[0])
            st = st * cd_ref[0] + upd
        st_ref[...] = st
        return carry

    lax.fori_loop(0, nchunk // unroll, body, 0)


def _ret_tables(reverse):
    c = RET_CHUNK
    lg = jnp.log(1.0 - 2.0 ** (-5.0 - jnp.arange(RET_HEADS, dtype=F32)))
    idx = jnp.arange(c, dtype=F32)
    if reverse:
        lg = lg[::-1]
        diff = idx[None, :] - idx[:, None]
        q_pow, k_pow = c - idx, idx
    else:
        diff = idx[:, None] - idx[None, :]
        q_pow, k_pow = idx + 1.0, c - 1.0 - idx
    lgh = lg[:, None, None]
    dm = jnp.where(diff >= 0, jnp.exp(lgh * jnp.maximum(diff, 0.0)), 0.0)
    qd = jnp.exp(lg[:, None] * q_pow[None, :])
    kd = jnp.exp(lg[:, None] * k_pow[None, :])
    cd = jnp.exp(lg * c)
    npair = RET_HEADS // 2
    dm = dm.reshape(npair, 2, c, c)
    qd = jnp.repeat(qd.reshape(npair, 2, c).transpose(0, 2, 1), RET_DV, axis=2)
    kd = jnp.repeat(kd.reshape(npair, 2, c).transpose(0, 2, 1), RET_DK, axis=2)
    cd = jnp.repeat(cd.reshape(npair, 1, 2), RET_DK, axis=2)
    return dm, qd, kd, cd


def _retention(q, k, v, rg, batch, seq):
    t = q.shape[0]
    ts = min(1024, seq)
    c = RET_CHUNK
    params = pltpu.CompilerParams(dimension_semantics=("parallel", "parallel", "arbitrary"),
                                  vmem_limit_bytes=VMEM_LIMIT)
    scratch = [pltpu.VMEM((2 * RET_DV, LANES), F32)]
    tab_specs = [pl.BlockSpec((1, 2, c, c), lambda b, p, i: (p, 0, 0, 0)),
                 pl.BlockSpec((1, c, 2 * RET_DV), lambda b, p, i: (p, 0, 0)),
                 pl.BlockSpec((1, c, LANES), lambda b, p, i: (p, 0, 0)),
                 pl.BlockSpec((1, 1, LANES), lambda b, p, i: (p, 0, 0))]
    ns, narrow, wide = _scan_specs(batch, seq, ts, False)
    o_f = pl.pallas_call(
        functools.partial(_ret_kernel, ts=ts, reverse=False),
        grid=(batch, RET_HEADS // 2, ns),
        in_specs=[narrow, narrow, wide] + tab_specs,
        out_specs=wide,
        out_shape=jax.ShapeDtypeStruct((t, RET_HEADS * RET_DV), F32),
        scratch_shapes=scratch, compiler_params=params, name="ret_fwd",
    )(q, k, v, *_ret_tables(False))
    ns, narrow, wide = _scan_specs(batch, seq, ts, True)
    return pl.pallas_call(
        functools.partial(_ret_kernel, ts=ts, reverse=True),
        grid=(batch, RET_HEADS // 2, ns),
        in_specs=[narrow, narrow, wide] + tab_specs + [wide, wide],
        out_specs=wide,
        out_shape=jax.ShapeDtypeStruct((t, RET_HEADS * RET_DV), BF16),
        scratch_shapes=scratch, compiler_params=params, name="ret_bwd",
    )(q, k, v, *_ret_tables(True), o_f, rg)


def _merge_mlp_kernel(x_ref, oa_ref, og_ref, or_ref, gmix_ref, gmlp_ref, gfin_ref, wm_ref, wb_ref, wo_ref,
                      wu_ref, wd_ref, y_ref, *, final):
    x = x_ref[...]
    xn = _rms(x, gmix_ref[...]).astype(BF16)
    merged = None
    for i, b_ref in enumerate((oa_ref, og_ref, or_ref)):
        gate = jax.nn.sigmoid(_dot(xn, wm_ref[:, i * D_MODEL:(i + 1) * D_MODEL]))
        term = gate * _dot(b_ref[...], wb_ref[i])
        merged = term if merged is None else merged + term
    h = x + _dot(merged.astype(BF16), wo_ref[...])
    hn = _rms(h, gmlp_ref[...]).astype(BF16)
    y = h
    for j in range(D_FF // D_MODEL):
        u = jnp.maximum(_dot(hn, wu_ref[:, j * D_MODEL:(j + 1) * D_MODEL]), 0.0)
        y = y + _dot((u * u).astype(BF16), wd_ref[j * D_MODEL:(j + 1) * D_MODEL, :])
    if final:
        y = _rms(y, gfin_ref[...])
    y_ref[...] = y


def _merge_mlp(x2d, oa, og, o_r, gmix, gmlp, gfin, wm, wb, wo, wu, wd, final):
    t = x2d.shape[0]
    tm = min(512, t)
    row = lambda w: pl.BlockSpec((tm, w), lambda i: (i, 0))
    vec = _const_spec((1, D_MODEL))
    return pl.pallas_call(
        functools.partial(_merge_mlp_kernel, final=final),
        grid=(t // tm,),
        in_specs=[row(D_MODEL), row(BRANCH_W), row(BRANCH_W), row(BRANCH_W), vec, vec, vec,
                  _const_spec(wm.shape), _const_spec(wb.shape), _const_spec(wo.shape),
                  _const_spec(wu.shape), _const_spec(wd.shape)],
        out_specs=row(D_MODEL),
        out_shape=jax.ShapeDtypeStruct((t, D_MODEL), F32),
        compiler_params=pltpu.CompilerParams(dimension_semantics=("parallel",), vmem_limit_bytes=VMEM_LIMIT),
        name="merge_mlp",
    )(x2d, oa, og, o_r, gmix, gmlp, gfin, wm, wb, wo, wu, wd)


def _rope_tables(seq):
    pos = jnp.arange(seq)
    axis_dim = HEAD_DIM // 2
    inv = ROPE_THETA ** (-jnp.arange(0, axis_dim, 2, dtype=F32) / axis_dim)
    ang_r = (pos // GRID_W).astype(F32)[:, None] * inv[None, :]
    ang_c = (pos % GRID_W).astype(F32)[:, None] * inv[None, :]
    ang = jnp.concatenate([ang_r, ang_r, ang_c, ang_c], axis=-1)
    ang = jnp.concatenate([ang, ang], axis=-1)
    first = (jnp.arange(LANES) % axis_dim) < axis_dim // 2
    cos, sin = jnp.cos(ang), jnp.sin(ang)
    return cos, jnp.where(first, -sin, 0.0), jnp.where(first, 0.0, sin)


def _prep_layer(w_in, gla_w_gate, gla_b_gate, q_gain, k_gain, gla_gain, w_branch, w_merge, w_out, w_up, w_down):
    splits = (512, 128, 128, 256, 256, 512, 32, 512, 256, 256, 512, 512)
    names = ("aq", "ak", "av", "gq", "gk", "gv", "ga", "gg", "rq", "rk", "rv", "rg")
    cols, start = {}, 0
    for n, w in zip(names, splits):
        cols[n] = w_in[:, start:start + w]
        start += w
    cols["ga"] = jnp.pad(cols["ga"], ((0, 0), (0, _W_GA - 2 * GLA_GATE_RANK)))
    order = sorted(_OFF, key=lambda n: _OFF[n][0])
    w_in_r = jnp.concatenate([cols[n] for n in order], axis=1).astype(BF16)
    nk = GLA_HEADS * GLA_DK
    r = GLA_GATE_RANK
    wg = jnp.zeros((_W_GA, 2 * nk), F32)
    wg = wg.at[:r, :nk].set(gla_w_gate[0]).at[r:2 * r, nk:].set(gla_w_gate[1]).astype(BF16)
    bg = jnp.concatenate([gla_b_gate[0], gla_b_gate[1]])[None, :]
    two = lambda g: jnp.concatenate([g, g])[None, :]
    att_shift = jnp.ceil(1.02 * ATT_Q_SCALE * HEAD_DIM * jnp.max(jnp.abs(q_gain)) * jnp.max(jnp.abs(k_gain)))
    return dict(w_in=w_in_r, wg=wg, bg=bg, qg=two(q_gain), kg=two(k_gain), gla_gain=gla_gain[None, :],
                att_shift=att_shift.reshape(1),
                wb=w_branch.astype(BF16), wm=w_merge.astype(BF16), wo=w_out.astype(BF16),
                wu=w_up.astype(BF16), wd=w_down.astype(BF16))


def _trunk(x, layers, norm_mix, norm_mlp, norm_final):
    batch, seq, _ = x.shape
    x2d = x.reshape(batch * seq, D_MODEL)
    cos, sa, sb = _rope_tables(seq)
    seg = (jnp.arange(LANES)[:, None] // HEAD_DIM == jnp.arange(LANES)[None, :] // HEAD_DIM).astype(BF16)
    depth = len(layers)
    for l, w in enumerate(layers):
        gmix = norm_mix[l][None, :]
        (aq, ak, av, gq, gk, gv, gg, rq, rk, rv, rg, la_f, la_b) = _proj(
            x2d, seq, gmix, w["w_in"], w["wg"], w["bg"], w["qg"], w["kg"], cos, sa, sb, seg)
        o_att = _attention(aq, ak, av, w["att_shift"], batch, seq)
        o_gla = _gla(gq, gk, gv, la_f, la_b, gg, w["gla_gain"], batch, seq)
        o_ret = _retention(rq, rk, rv, rg, batch, seq)
        x2d = _merge_mlp(x2d, o_att, o_gla, o_ret, gmix, norm_mlp[l][None, :], norm_final[None, :],
                         w["wm"], w["wb"], w["wo"], w["wu"], w["wd"], final=(l == depth - 1))
    return x2d.reshape(batch, seq, D_MODEL)


def kernel(x_prompt, x_sample, norm_mix, norm_mlp, w_in, attn_q_norm, attn_k_norm, gla_w_gate, gla_b_gate,
           gla_out_norm, w_branch, w_merge, w_out, w_up, w_down, norm_final):
    layers = [_prep_layer(w_in[l], gla_w_gate[l], gla_b_gate[l], attn_q_norm[l], attn_k_norm[l],
                          gla_out_norm[l], w_branch[l], w_merge[l], w_out[l], w_up[l], w_down[l])
              for l in range(w_in.shape[0])]
    y_prompt = _trunk(x_prompt, layers, norm_mix, norm_mlp, norm_final)
    y_sample = _trunk(x_sample, layers, norm_mix, norm_mlp, norm_final)
    return (y_prompt, y_sample)
```

```python
import functools
import math

import jax
import jax.numpy as jnp
from jax import lax
from jax.experimental import pallas as pl
from jax.experimental.pallas import tpu as pltpu

F32 = jnp.float32
BF16 = jnp.bfloat16

D_MODEL = 1024
GRID_W = 64
HEAD_DIM = 64
ATT_HEADS = 8
ATT_KV_HEADS = 2
ROPE_THETA = 10000.0
GLA_HEADS = 4
GLA_DK = 64
GLA_DV = 128
GLA_GATE_RANK = 16
GLA_GATE_TAU = 16.0
GLA_CHUNK = 64
RET_HEADS = 4
RET_DK = 64
RET_DV = 128
RET_CHUNK = 128
BRANCH_W = 512
N_BRANCH = 3
D_FF = 4 * D_MODEL
EPS = 1e-6

ATT_GROUPS = ATT_HEADS // ATT_KV_HEADS
ATT_Q_SCALE = HEAD_DIM ** -0.5 * math.log2(math.e)
ATT_MAX_SHIFT = 40.0
SCAN_UNROLL = 8

LANES = 128
VMEM_LIMIT = 56 * 1024 * 1024

_W_AQ, _W_AK, _W_AV = 512, 128, 128
_W_GQ, _W_GK, _W_GV, _W_GG = 256, 256, 512, 512
_W_RQ, _W_RK, _W_RV, _W_RG = 256, 256, 512, 512
_W_GA = LANES
_OFF = {}
_o = 0
for _n, _w in (("aq", _W_AQ), ("ak", _W_AK), ("av", _W_AV), ("gq", _W_GQ), ("gk", _W_GK), ("gv", _W_GV),
               ("gg", _W_GG), ("rq", _W_RQ), ("rk", _W_RK), ("rv", _W_RV), ("rg", _W_RG), ("ga", _W_GA)):
    _OFF[_n] = (_o, _w)
    _o += _w
IN_WIDTH_PADDED = _o


def _const_spec(shape):
    nd = len(shape)
    return pl.BlockSpec(shape, lambda *_: (0,) * nd, pipeline_mode=pl.Buffered(1))


def _dot(a, b):
    return jnp.dot(a, b, preferred_element_type=F32)


def _dot_nt(a, b):
    return lax.dot_general(a, b, (((1,), (1,)), ((), ())), preferred_element_type=F32)


def _dot_tn(a, b):
    return lax.dot_general(a, b, (((0,), (0,)), ((), ())), preferred_element_type=F32)


def _split3(x):
    hi = x.astype(BF16)
    r = x - hi.astype(F32)
    mid = r.astype(BF16)
    lo = (r - mid.astype(F32)).astype(BF16)
    return hi, mid, lo


def _rms(x, gain):
    return x * lax.rsqrt(jnp.mean(x * x, axis=-1, keepdims=True) + EPS) * gain


def _rope(x, cos, sin_a, sin_b):
    return x * cos + pltpu.roll(x, LANES - 16, 1) * sin_a + pltpu.roll(x, 16, 1) * sin_b


def _proj_kernel(x_ref, gmix_ref, w_ref, wg_ref, bg_ref, qg_ref, kg_ref, cos_ref, sa_ref, sb_ref,
                 aq_ref, ak_ref, av_ref, gq_ref, gk_ref, gv_ref, gg_ref, rq_ref, rk_ref, rv_ref, rg_ref,
                 laf_ref, lab_ref):
    xn = _rms(x_ref[...], gmix_ref[...]).astype(BF16)
    cos, sa, sb = cos_ref[...], sa_ref[...], sb_ref[...]

    def proj(name):
        off, w = _OFF[name]
        return _dot(xn, w_ref[:, off:off + w])

    lo_half = lax.broadcasted_iota(jnp.int32, (x_ref.shape[0], LANES), 1) < HEAD_DIM

    def head_norm(t, gain):
        y = t * t
        ss_lo = jnp.sum(jnp.where(lo_half, y, 0.0), axis=-1, keepdims=True)
        ss_hi = jnp.sum(jnp.where(lo_half, 0.0, y), axis=-1, keepdims=True)
        ss = jnp.where(lo_half, ss_lo, ss_hi)
        return t * lax.rsqrt(ss * (1.0 / HEAD_DIM) + EPS) * gain

    def per_tile(p, out_ref, fn):
        for j in range(p.shape[1] // LANES):
            out_ref[:, j * LANES:(j + 1) * LANES] = fn(p[:, j * LANES:(j + 1) * LANES]).astype(out_ref.dtype)

    per_tile(proj("aq"), aq_ref, lambda t: _rope(head_norm(t, qg_ref[...]), cos, sa, sb) * ATT_Q_SCALE)
    akv = _dot(xn, w_ref[:, _OFF["ak"][0]:_OFF["av"][0] + _W_AV])
    ak_ref[...] = _rope(head_norm(akv[:, :_W_AK], kg_ref[...]), cos, sa, sb).astype(BF16)
    av_ref[:, :_W_AV] = akv[:, _W_AK:].astype(BF16)
    av_ref[:, _W_AV:] = jnp.ones((av_ref.shape[0], LANES), BF16)
    gq_ref[...] = (proj("gq") * (GLA_DK ** -0.5)).astype(BF16)
    gk_ref[...] = proj("gk").astype(BF16)
    gv_ref[...] = proj("gv").astype(BF16)
    silu = lambda t: t * jax.nn.sigmoid(t)
    gg_ref[...] = silu(proj("gg")).astype(BF16)
    per_tile(proj("rq"), rq_ref, lambda t: _rope(t, cos, sa, sb))
    per_tile(proj("rk"), rk_ref, lambda t: _rope(t, cos, sa, sb) * (RET_DK ** -0.5))
    rv_ref[...] = proj("rv").astype(BF16)
    rg_ref[...] = silu(proj("rg")).astype(BF16)
    z = proj("ga").astype(BF16)
    pre = _dot(z, wg_ref[...]) + bg_ref[...]
    la = (jnp.minimum(pre, 0.0) - jnp.log1p(jnp.exp(-jnp.abs(pre)))) * (1.0 / GLA_GATE_TAU)
    nk = GLA_HEADS * GLA_DK
    laf_ref[...] = la[:, :nk]
    lab_ref[...] = la[:, nk:]


def _proj(x2d, seq, gmix, w_in, wg, bg, qg, kg, cos, sa, sb):
    t = x2d.shape[0]
    tm = min(512, seq)
    nblk = seq // tm
    row = lambda w, dt: (pl.BlockSpec((tm, w), lambda i: (i, 0)), jax.ShapeDtypeStruct((t, w), dt))
    outs = [row(_W_AQ, BF16), row(_W_AK, BF16), row(_W_AV + LANES, BF16), row(_W_GQ, BF16), row(_W_GK, BF16),
            row(_W_GV, BF16), row(_W_GG, BF16), row(_W_RQ, BF16), row(_W_RK, BF16), row(_W_RV, BF16),
            row(_W_RG, BF16), row(GLA_HEADS * GLA_DK, F32), row(GLA_HEADS * GLA_DK, F32)]
    tab = pl.BlockSpec((tm, LANES), lambda i: (i % nblk, 0))
    return pl.pallas_call(
        _proj_kernel,
        grid=(t // tm,),
        in_specs=[pl.BlockSpec((tm, D_MODEL), lambda i: (i, 0)), _const_spec((1, D_MODEL)),
                  _const_spec(w_in.shape), _const_spec(wg.shape), _const_spec(bg.shape),
                  _const_spec((1, LANES)), _const_spec((1, LANES)), tab, tab, tab],
        out_specs=[o[0] for o in outs],
        out_shape=[o[1] for o in outs],
        compiler_params=pltpu.CompilerParams(dimension_semantics=("parallel",), vmem_limit_bytes=VMEM_LIMIT),
        name="proj",
    )(x2d, gmix, w_in, wg, bg, qg, kg, cos, sa, sb)


def _stack_query_heads(q_ref, tq):
    lo_half = lax.broadcasted_iota(jnp.int32, (tq, LANES), 1) < HEAD_DIM
    qs = []
    for h in range(ATT_HEADS):
        blk = q_ref[:, (h // 2) * LANES:(h // 2 + 1) * LANES].astype(F32)
        src_half, dst_half = h % 2, h // ATT_GROUPS
        if src_half != dst_half:
            blk = pltpu.roll(blk, HEAD_DIM, 1)
        keep = lo_half if dst_half == 0 else jnp.logical_not(lo_half)
        qs.append(jnp.where(keep, blk, 0.0).astype(BF16))
    return jnp.concatenate(qs, axis=0)


def _store_heads(acc_ref, o_ref, tq):
    lo_half = lax.broadcasted_iota(jnp.int32, (tq, LANES), 1) < HEAD_DIM
    out = acc_ref[:, :LANES] / acc_ref[:, LANES:]
    for j in range(ATT_HEADS // 2):
        a = out[(2 * j) * tq:(2 * j + 1) * tq]
        b = out[(2 * j + 1) * tq:(2 * j + 2) * tq]
        if (2 * j) // ATT_GROUPS == 0:
            b = pltpu.roll(b, HEAD_DIM, 1)
        else:
            a = pltpu.roll(a, HEAD_DIM, 1)
        o_ref[:, j * LANES:(j + 1) * LANES] = jnp.where(lo_half, a, b).astype(o_ref.dtype)


def _attn_shifted_kernel(shift_ref, q_ref, k_ref, v_ref, o_ref, acc_ref, *, tq, tk, seq):
    q = _stack_query_heads(q_ref, tq)
    shift = shift_ref[0]
    acc_ref[...] = jnp.zeros(acc_ref.shape, F32)

    def body(c, carry):
        start = pl.multiple_of(c * tk, tk)
        p = jnp.exp2(_dot_nt(q, k_ref[pl.ds(start, tk), :]) - shift).astype(BF16)
        acc_ref[...] += _dot(p, v_ref[pl.ds(start, tk), :])
        return carry

    lax.fori_loop(0, seq // tk, body, 0, unroll=4)
    _store_heads(acc_ref, o_ref, tq)


def _attn_online_kernel(q_ref, k_ref, v_ref, o_ref, acc_ref, m_ref, *, tq, tk, seq):
    q = _stack_query_heads(q_ref, tq)
    m_ref[...] = jnp.full(m_ref.shape, -jnp.inf, F32)
    acc_ref[...] = jnp.zeros(acc_ref.shape, F32)

    def body(c, carry):
        start = pl.multiple_of(c * tk, tk)
        s = _dot_nt(q, k_ref[pl.ds(start, tk), :])
        m_old = m_ref[...]
        m_new = jnp.maximum(m_old, jnp.max(s, axis=-1, keepdims=True))
        p = jnp.exp2(s - m_new).astype(BF16)
        acc_ref[...] = jnp.exp2(m_old - m_new) * acc_ref[...] + _dot(p, v_ref[pl.ds(start, tk), :])
        m_ref[...] = m_new
        return carry

    lax.fori_loop(0, seq // tk, body, 0)
    _store_heads(acc_ref, o_ref, tq)


def _attention(q, k, v, shift, batch, seq):
    t = q.shape[0]
    tq = 128
    tk = min(512, seq)
    nq = seq // tq
    m_rows = ATT_HEADS * tq
    row_spec = pl.BlockSpec((tq, _W_AQ), lambda b, i: (b * nq + i, 0))
    k_spec = pl.BlockSpec((seq, LANES), lambda b, i: (b, 0))
    v_spec = pl.BlockSpec((seq, 2 * LANES), lambda b, i: (b, 0))
    common = dict(
        grid=(batch, nq),
        out_specs=row_spec,
        out_shape=jax.ShapeDtypeStruct((t, _W_AQ), BF16),
        compiler_params=pltpu.CompilerParams(dimension_semantics=("parallel", "parallel"),
                                             vmem_limit_bytes=VMEM_LIMIT))
    acc = pltpu.VMEM((m_rows, 2 * LANES), F32)

    def shifted(q, k, v):
        return pl.pallas_call(
            functools.partial(_attn_shifted_kernel, tq=tq, tk=tk, seq=seq),
            in_specs=[pl.BlockSpec(memory_space=pltpu.SMEM), row_spec, k_spec, v_spec],
            scratch_shapes=[acc], name="attention", **common)(shift, q, k, v)

    def online(q, k, v):
        return pl.pallas_call(
            functools.partial(_attn_online_kernel, tq=tq, tk=tk, seq=seq),
            in_specs=[row_spec, k_spec, v_spec],
            scratch_shapes=[acc, pltpu.VMEM((m_rows, 1), F32)], name="attention_online", **common)(q, k, v)

    return lax.cond(shift[0] <= ATT_MAX_SHIFT, shifted, online, q, k, v)


def _lane_head(chunk):
    return lax.broadcasted_iota(jnp.int32, (chunk, LANES), 1) // GLA_DK


def _gla_kernel(*refs, ts, reverse):
    if reverse:
        q_ref, k_ref, v_ref, la_ref, of_ref, gg_ref, gain_ref, o_ref, st_ref = refs
    else:
        q_ref, k_ref, v_ref, la_ref, o_ref, st_ref = refs
    c = GLA_CHUNK
    nchunk = ts // c

    @pl.when(pl.program_id(2) == 0)
    def _():
        st_ref[...] = jnp.zeros(st_ref.shape, F32)

    lane_head = _lane_head(c)
    ri = lax.broadcasted_iota(jnp.int32, (c, c), 0)
    ci = lax.broadcasted_iota(jnp.int32, (c, c), 1)
    causal = (ri <= ci) if reverse else (ri >= ci)
    cum = causal.astype(BF16)
    mid = c // 2 if reverse else c // 2 - 1
    last = 0 if reverse else c - 1

    def local_parts(rows_list):
        cums = [_dot(cum, jnp.concatenate(_split3(la_ref[rows, :]), axis=1)) for rows in rows_list]
        scores, tails = [], []
        for rows, parts in zip(rows_list, cums):
            qf = q_ref[rows, :].astype(F32)
            kf = k_ref[rows, :].astype(F32)
            b = parts[:, :LANES] + parts[:, LANES:2 * LANES] + parts[:, 2 * LANES:]
            b_ref = b[mid:mid + 1, :]
            b_last = b[last:last + 1, :]
            q_in = qf * jnp.exp(b - b_ref)
            k_in = (kf * jnp.exp(b_ref - b)).astype(BF16)
            scores.append([_dot_nt(jnp.where(lane_head == hh, q_in, 0.0).astype(BF16), k_in) for hh in range(2)])
            kd = kf * jnp.exp(b_last - b)
            kd = [jnp.where(lane_head == hh, kd, 0.0).astype(BF16) for hh in range(2)]
            tails.append(((qf * jnp.exp(b)).astype(BF16), jnp.exp(b_last), kd))
        out = []
        for rows, ss, (q_dec, decay, kd) in zip(rows_list, scores, tails):
            vc = [v_ref[rows, hh * GLA_DV:(hh + 1) * GLA_DV] for hh in range(2)]
            intra = [_dot(jnp.where(causal, s, 0.0).astype(BF16), vc[hh]) for hh, s in enumerate(ss)]
            upd = jnp.concatenate([_dot_tn(vc[hh], kd[hh]) for hh in range(2)], axis=0)
            out.append((jnp.concatenate(intra, axis=1), q_dec, decay, upd))
        return out

    def finish(rows, o):
        if not reverse:
            o_ref[rows, :] = o
            return
        o = o + of_ref[rows, :]
        gate = gg_ref[rows, :].astype(F32)
        for hh in range(2):
            sl = slice(hh * GLA_DV, (hh + 1) * GLA_DV)
            o_ref[rows, sl] = (_rms(o[:, sl], gain_ref[...]) * gate[:, sl]).astype(o_ref.dtype)

    unroll = min(SCAN_UNROLL, nchunk)

    def body(n, carry):
        rows = []
        for u in range(unroll):
            idx = n * unroll + u
            ch = (nchunk - 1 - idx) if reverse else idx
            rows.append(pl.ds(pl.multiple_of(ch * c, c), c))
        local = local_parts(rows)
        st = st_ref[...]
        for r, (intra, q_dec, decay, upd) in zip(rows, local):
            finish(r, intra + _dot_nt(q_dec, st.astype(BF16)))
            st = st * decay + upd
        st_ref[...] = st
        return carry

    lax.fori_loop(0, nchunk // unroll, body, 0)


def _scan_specs(batch, seq, ts, reverse):
    ns = seq // ts
    blk = (lambda i: ns - 1 - i) if reverse else (lambda i: i)
    narrow = pl.BlockSpec((ts, LANES), lambda b, p, i: (b * ns + blk(i), p))
    wide = pl.BlockSpec((ts, 2 * GLA_DV), lambda b, p, i: (b * ns + blk(i), p))
    return ns, narrow, wide


def _gla(q, k, v, la_f, la_b, gg, gain, batch, seq):
    t = q.shape[0]
    ts = min(1024, seq)
    params = pltpu.CompilerParams(dimension_semantics=("parallel", "parallel", "arbitrary"),
                                  vmem_limit_bytes=VMEM_LIMIT)
    scratch = [pltpu.VMEM((2 * GLA_DV, LANES), F32)]
    ns, narrow, wide = _scan_specs(batch, seq, ts, False)
    o_f = pl.pallas_call(
        functools.partial(_gla_kernel, ts=ts, reverse=False),
        grid=(batch, GLA_HEADS // 2, ns),
        in_specs=[narrow, narrow, wide, narrow],
        out_specs=wide,
        out_shape=jax.ShapeDtypeStruct((t, GLA_HEADS * GLA_DV), F32),
        scratch_shapes=scratch, compiler_params=params, name="gla_fwd",
    )(q, k, v, la_f)
    ns, narrow, wide = _scan_specs(batch, seq, ts, True)
    return pl.pallas_call(
        functools.partial(_gla_kernel, ts=ts, reverse=True),
        grid=(batch, GLA_HEADS // 2, ns),
        in_specs=[narrow, narrow, wide, narrow, wide, wide, _const_spec((1, GLA_DV))],
        out_specs=wide,
        out_shape=jax.ShapeDtypeStruct((t, GLA_HEADS * GLA_DV), BF16),
        scratch_shapes=scratch, compiler_params=params, name="gla_bwd",
    )(q, k, v, la_b, o_f, gg, gain)


def _ret_kernel(*refs, ts, reverse):
    if reverse:
        q_ref, k_ref, v_ref, dm_ref, qd_ref, kd_ref, cd_ref, of_ref, rg_ref, o_ref, st_ref = refs
    else:
        q_ref, k_ref, v_ref, dm_ref, qd_ref, kd_ref, cd_ref, o_ref, st_ref = refs
    c = RET_CHUNK
    nchunk = ts // c

    @pl.when(pl.program_id(2) == 0)
    def _():
        st_ref[...] = jnp.zeros(st_ref.shape, F32)

    lane_head = _lane_head(c)

    def local_parts(rows_list):
        scores = []
        for rows in rows_list:
            qc = q_ref[rows, :]
            kc = k_ref[rows, :]
            scores.append([_dot_nt(jnp.where(lane_head == hh, qc, jnp.zeros_like(qc)), kc) for hh in range(2)])
        out = []
        for rows, ss in zip(rows_list, scores):
            vc = [v_ref[rows, hh * RET_DV:(hh + 1) * RET_DV] for hh in range(2)]
            intra = [_dot((s * dm_ref[0, hh]).astype(BF16), vc[hh]) for hh, s in enumerate(ss)]
            kf = k_ref[rows, :].astype(F32)
            upd = jnp.concatenate([_dot_tn(vc[hh], (kf * kd_ref[0, hh]).astype(BF16)) for hh in range(2)], axis=0)
            out.append((jnp.concatenate(intra, axis=1), q_ref[rows, :], upd))
        return out

    def finish(rows, o):
        if not reverse:
            o_ref[rows, :] = o
            return
        o = o + of_ref[rows, :]
        gate = rg_ref[rows, :].astype(F32)
        for hh in range(2):
            sl = slice(hh * RET_DV, (hh + 1) * RET_DV)
            xc = o[:, sl] - jnp.mean(o[:, sl], axis=-1, keepdims=True)
            y = xc * lax.rsqrt(jnp.mean(xc * xc, axis=-1, keepdims=True) + EPS)
            o_ref[rows, sl] = (y * gate[:, sl]).astype(o_ref.dtype)

    unroll = min(SCAN_UNROLL, nchunk)

    def body(n, carry):
        rows = []
        for u in range(unroll):
            idx = n * unroll + u
            ch = (nchunk - 1 - idx) if reverse else idx
            rows.append(pl.ds(pl.multiple_of(ch * c, c), c))
        local = local_parts(rows)
        st = st_ref[...]
        for r, (intra, qc, upd) in zip(rows, local):
            finish(r, intra + _dot_nt(qc, st.astype(BF16)) * qd_ref[0])
            st = st * cd_ref[0] + upd
        st_ref[...] = st
        return carry

    lax.fori_loop(0, nchunk // unroll, body, 0)


def _ret_tables(reverse):
    c = RET_CHUNK
    lg = jnp.log(1.0 - 2.0 ** (-5.0 - jnp.arange(RET_HEADS, dtype=F32)))
    idx = jnp.arange(c, dtype=F32)
    if reverse:
        lg = lg[::-1]
        diff = idx[None, :] - idx[:, None]
        q_pow, k_pow = c - idx, idx
    else:
        diff = idx[:, None] - idx[None, :]
        q_pow, k_pow = idx + 1.0, c - 1.0 - idx
    lgh = lg[:, None, None]
    dm = jnp.where(diff >= 0, jnp.exp(lgh * jnp.maximum(diff, 0.0)), 0.0)
    qd = jnp.exp(lg[:, None] * q_pow[None, :])
    kd = jnp.exp(lg[:, None] * k_pow[None, :])
    cd = jnp.exp(lg * c)
    npair = RET_HEADS // 2
    dm = dm.reshape(npair, 2, c, c)
    qd = jnp.repeat(qd.reshape(npair, 2, c).transpose(0, 2, 1), RET_DV, axis=2)
    kd = jnp.repeat(kd.reshape(npair, 2, c).transpose(0, 2, 1), RET_DK, axis=2)
    own = jnp.arange(LANES)[None, :] // RET_DK == jnp.arange(2)[:, None]
    kd = jnp.where(own[None, :, None, :], kd[:, None], 0.0)
    cd = jnp.repeat(cd.reshape(npair, 1, 2), RET_DK, axis=2)
    return dm, qd, kd, cd


def _retention(q, k, v, rg, batch, seq):
    t = q.shape[0]
    ts = min(1024, seq)
    c = RET_CHUNK
    params = pltpu.CompilerParams(dimension_semantics=("parallel", "parallel", "arbitrary"),
                                  vmem_limit_bytes=VMEM_LIMIT)
    scratch = [pltpu.VMEM((2 * RET_DV, LANES), F32)]
    tab_specs = [pl.BlockSpec((1, 2, c, c), lambda b, p, i: (p, 0, 0, 0)),
                 pl.BlockSpec((1, c, 2 * RET_DV), lambda b, p, i: (p, 0, 0)),
                 pl.BlockSpec((1, 2, c, LANES), lambda b, p, i: (p, 0, 0, 0)),
                 pl.BlockSpec((1, 1, LANES), lambda b, p, i: (p, 0, 0))]
    ns, narrow, wide = _scan_specs(batch, seq, ts, False)
    o_f = pl.pallas_call(
        functools.partial(_ret_kernel, ts=ts, reverse=False),
        grid=(batch, RET_HEADS // 2, ns),
        in_specs=[narrow, narrow, wide] + tab_specs,
        out_specs=wide,
        out_shape=jax.ShapeDtypeStruct((t, RET_HEADS * RET_DV), F32),
        scratch_shapes=scratch, compiler_params=params, name="ret_fwd",
    )(q, k, v, *_ret_tables(False))
    ns, narrow, wide = _scan_specs(batch, seq, ts, True)
    return pl.pallas_call(
        functools.partial(_ret_kernel, ts=ts, reverse=True),
        grid=(batch, RET_HEADS // 2, ns),
        in_specs=[narrow, narrow, wide] + tab_specs + [wide, wide],
        out_specs=wide,
        out_shape=jax.ShapeDtypeStruct((t, RET_HEADS * RET_DV), BF16),
        scratch_shapes=scratch, compiler_params=params, name="ret_bwd",
    )(q, k, v, *_ret_tables(True), o_f, rg)


def _merge_mlp_kernel(x_ref, oa_ref, og_ref, or_ref, gmix_ref, gmlp_ref, gfin_ref, wm_ref, wb_ref, wo_ref,
                      wu_ref, wd_ref, y_ref, *, final):
    x = x_ref[...]
    xn = _rms(x, gmix_ref[...]).astype(BF16)
    merged = None
    for i, b_ref in enumerate((oa_ref, og_ref, or_ref)):
        gate = jax.nn.sigmoid(_dot(xn, wm_ref[:, i * D_MODEL:(i + 1) * D_MODEL]))
        term = gate * _dot(b_ref[...], wb_ref[i])
        merged = term if merged is None else merged + term
    h = x + _dot(merged.astype(BF16), wo_ref[...])
    hn = _rms(h, gmlp_ref[...]).astype(BF16)
    y = h
    for j in range(D_FF // D_MODEL):
        u = jnp.maximum(_dot(hn, wu_ref[:, j * D_MODEL:(j + 1) * D_MODEL]), 0.0)
        y = y + _dot((u * u).astype(BF16), wd_ref[j * D_MODEL:(j + 1) * D_MODEL, :])
    if final:
        y = _rms(y, gfin_ref[...])
    y_ref[...] = y


def _merge_mlp(x2d, oa, og, o_r, gmix, gmlp, gfin, wm, wb, wo, wu, wd, final):
    t = x2d.shape[0]
    tm = min(512, t)
    row = lambda w: pl.BlockSpec((tm, w), lambda i: (i, 0))
    vec = _const_spec((1, D_MODEL))
    return pl.pallas_call(
        functools.partial(_merge_mlp_kernel, final=final),
        grid=(t // tm,),
        in_specs=[row(D_MODEL), row(BRANCH_W), row(BRANCH_W), row(BRANCH_W), vec, vec, vec,
                  _const_spec(wm.shape), _const_spec(wb.shape), _const_spec(wo.shape),
                  _const_spec(wu.shape), _const_spec(wd.shape)],
        out_specs=row(D_MODEL),
        out_shape=jax.ShapeDtypeStruct((t, D_MODEL), F32),
        compiler_params=pltpu.CompilerParams(dimension_semantics=("parallel",), vmem_limit_bytes=VMEM_LIMIT),
        name="merge_mlp",
    )(x2d, oa, og, o_r, gmix, gmlp, gfin, wm, wb, wo, wu, wd)


def _rope_tables(seq):
    pos = jnp.arange(seq)
    axis_dim = HEAD_DIM // 2
    inv = ROPE_THETA ** (-jnp.arange(0, axis_dim, 2, dtype=F32) / axis_dim)
    ang_r = (pos // GRID_W).astype(F32)[:, None] * inv[None, :]
    ang_c = (pos % GRID_W).astype(F32)[:, None] * inv[None, :]
    ang = jnp.concatenate([ang_r, ang_r, ang_c, ang_c], axis=-1)
    ang = jnp.concatenate([ang, ang], axis=-1)
    first = (jnp.arange(LANES) % axis_dim) < axis_dim // 2
    cos, sin = jnp.cos(ang), jnp.sin(ang)
    return cos, jnp.where(first, -sin, 0.0), jnp.where(first, 0.0, sin)


def _prep_layer(w_in, gla_w_gate, gla_b_gate, q_gain, k_gain, gla_gain, w_branch, w_merge, w_out, w_up, w_down):
    splits = (512, 128, 128, 256, 256, 512, 32, 512, 256, 256, 512, 512)
    names = ("aq", "ak", "av", "gq", "gk", "gv", "ga", "gg", "rq", "rk", "rv", "rg")
    cols, start = {}, 0
    for n, w in zip(names, splits):
        cols[n] = w_in[:, start:start + w]
        start += w
    cols["ga"] = jnp.pad(cols["ga"], ((0, 0), (0, _W_GA - 2 * GLA_GATE_RANK)))
    order = sorted(_OFF, key=lambda n: _OFF[n][0])
    w_in_r = jnp.concatenate([cols[n] for n in order], axis=1).astype(BF16)
    nk = GLA_HEADS * GLA_DK
    r = GLA_GATE_RANK
    wg = jnp.zeros((_W_GA, 2 * nk), F32)
    wg = wg.at[:r, :nk].set(gla_w_gate[0]).at[r:2 * r, nk:].set(gla_w_gate[1]).astype(BF16)
    bg = jnp.concatenate([gla_b_gate[0], gla_b_gate[1]])[None, :]
    two = lambda g: jnp.concatenate([g, g])[None, :]
    att_shift = jnp.ceil(1.02 * ATT_Q_SCALE * HEAD_DIM * jnp.max(jnp.abs(q_gain)) * jnp.max(jnp.abs(k_gain)))
    return dict(w_in=w_in_r, wg=wg, bg=bg, qg=two(q_gain), kg=two(k_gain), gla_gain=gla_gain[None, :],
                att_shift=att_shift.reshape(1),
                wb=w_branch.astype(BF16), wm=w_merge.astype(BF16), wo=w_out.astype(BF16),
                wu=w_up.astype(BF16), wd=w_down.astype(BF16))


def _trunk(x, layers, norm_mix, norm_mlp, norm_final):
    batch, seq, _ = x.shape
    x2d = x.reshape(batch * seq, D_MODEL)
    cos, sa, sb = _rope_tables(seq)
    depth = len(layers)
    for l, w in enumerate(layers):
        gmix = norm_mix[l][None, :]
        (aq, ak, av, gq, gk, gv, gg, rq, rk, rv, rg, la_f, la_b) = _proj(
            x2d, seq, gmix, w["w_in"], w["wg"], w["bg"], w["qg"], w["kg"], cos, sa, sb)
        o_att = _attention(aq, ak, av, w["att_shift"], batch, seq)
        o_gla = _gla(gq, gk, gv, la_f, la_b, gg, w["gla_gain"], batch, seq)
        o_ret = _retention(rq, rk, rv, rg, batch, seq)
        x2d = _merge_mlp(x2d, o_att, o_gla, o_ret, gmix, norm_mlp[l][None, :], norm_final[None, :],
                         w["wm"], w["wb"], w["wo"], w["wu"], w["wd"], final=(l == depth - 1))
    return x2d.reshape(batch, seq, D_MODEL)


def kernel(x_prompt, x_sample, norm_mix, norm_mlp, w_in, attn_q_norm, attn_k_norm, gla_w_gate, gla_b_gate,
           gla_out_norm, w_branch, w_merge, w_out, w_up, w_down, norm_final):
    layers = [_prep_layer(w_in[l], gla_w_gate[l], gla_b_gate[l], attn_q_norm[l], attn_k_norm[l],
                          gla_out_norm[l], w_branch[l], w_merge[l], w_out[l], w_up[l], w_down[l])
              for l in range(w_in.shape[0])]
    y_prompt = _trunk(x_prompt, layers, norm_mix, norm_mlp, norm_final)
    y_sample = _trunk(x_sample, layers, norm_mix, norm_mlp, norm_final)
    return (y_prompt, y_sample)
```

```python
import functools
import math

import jax
import jax.numpy as jnp
from jax import lax
from jax.experimental import pallas as pl
from jax.experimental.pallas import tpu as pltpu

F32 = jnp.float32
BF16 = jnp.bfloat16

D_MODEL = 1024
GRID_W = 64
HEAD_DIM = 64
ATT_HEADS = 8
ATT_KV_HEADS = 2
ROPE_THETA = 10000.0
GLA_HEADS = 4
GLA_DK = 64
GLA_DV = 128
GLA_GATE_RANK = 16
GLA_GATE_TAU = 16.0
GLA_CHUNK = 64
RET_HEADS = 4
RET_DK = 64
RET_DV = 128
RET_CHUNK = 128
BRANCH_W = 512
N_BRANCH = 3
D_FF = 4 * D_MODEL
EPS = 1e-6

ATT_GROUPS = ATT_HEADS // ATT_KV_HEADS
ATT_Q_SCALE = HEAD_DIM ** -0.5 * math.log2(math.e)
ATT_MAX_SHIFT = 40.0
SCAN_UNROLL = 16

LANES = 128
VMEM_LIMIT = 56 * 1024 * 1024

_W_AQ, _W_AK, _W_AV = 512, 128, 128
_W_GQ, _W_GK, _W_GV, _W_GG = 256, 256, 512, 512
_W_RQ, _W_RK, _W_RV, _W_RG = 256, 256, 512, 512
_W_GA = LANES
_OFF = {}
_o = 0
for _n, _w in (("aq", _W_AQ), ("ak", _W_AK), ("av", _W_AV), ("gq", _W_GQ), ("gk", _W_GK), ("gv", _W_GV),
               ("gg", _W_GG), ("rq", _W_RQ), ("rk", _W_RK), ("rv", _W_RV), ("rg", _W_RG), ("ga", _W_GA)):
    _OFF[_n] = (_o, _w)
    _o += _w
IN_WIDTH_PADDED = _o


def _const_spec(shape):
    nd = len(shape)
    return pl.BlockSpec(shape, lambda *_: (0,) * nd, pipeline_mode=pl.Buffered(1))


def _dot(a, b):
    return jnp.dot(a, b, preferred_element_type=F32)


def _dot_nt(a, b):
    return lax.dot_general(a, b, (((1,), (1,)), ((), ())), preferred_element_type=F32)


def _dot_tn(a, b):
    return lax.dot_general(a, b, (((0,), (0,)), ((), ())), preferred_element_type=F32)


def _split2(x):
    hi = x.astype(BF16)
    return hi, (x - hi.astype(F32)).astype(BF16)


def _rms(x, gain):
    return x * lax.rsqrt(jnp.mean(x * x, axis=-1, keepdims=True) + EPS) * gain


def _rope(x, cos, sin_a, sin_b):
    return x * cos + pltpu.roll(x, LANES - 16, 1) * sin_a + pltpu.roll(x, 16, 1) * sin_b


def _proj_kernel(x_ref, gmix_ref, w_ref, wg_ref, bg_ref, qg_ref, kg_ref, cos_ref, sa_ref, sb_ref,
                 aq_ref, ak_ref, av_ref, gq_ref, gk_ref, gv_ref, gg_ref, rq_ref, rk_ref, rv_ref, rg_ref,
                 laf_ref, lab_ref):
    xn = _rms(x_ref[...], gmix_ref[...]).astype(BF16)
    cos, sa, sb = cos_ref[...], sa_ref[...], sb_ref[...]

    def proj(name):
        off, w = _OFF[name]
        return _dot(xn, w_ref[:, off:off + w])

    lo_half = lax.broadcasted_iota(jnp.int32, (x_ref.shape[0], LANES), 1) < HEAD_DIM

    def head_norm(t, gain):
        y = t * t
        ss_lo = jnp.sum(jnp.where(lo_half, y, 0.0), axis=-1, keepdims=True)
        ss_hi = jnp.sum(jnp.where(lo_half, 0.0, y), axis=-1, keepdims=True)
        ss = jnp.where(lo_half, ss_lo, ss_hi)
        return t * lax.rsqrt(ss * (1.0 / HEAD_DIM) + EPS) * gain

    def per_tile(p, out_ref, fn):
        for j in range(p.shape[1] // LANES):
            out_ref[:, j * LANES:(j + 1) * LANES] = fn(p[:, j * LANES:(j + 1) * LANES]).astype(out_ref.dtype)

    per_tile(proj("aq"), aq_ref, lambda t: _rope(head_norm(t, qg_ref[...]), cos, sa, sb) * ATT_Q_SCALE)
    akv = _dot(xn, w_ref[:, _OFF["ak"][0]:_OFF["av"][0] + _W_AV])
    ak_ref[...] = _rope(head_norm(akv[:, :_W_AK], kg_ref[...]), cos, sa, sb).astype(BF16)
    av_ref[:, :_W_AV] = akv[:, _W_AK:].astype(BF16)
    av_ref[:, _W_AV:] = jnp.ones((av_ref.shape[0], LANES), BF16)
    gq_ref[...] = (proj("gq") * (GLA_DK ** -0.5)).astype(BF16)
    gk_ref[...] = proj("gk").astype(BF16)
    gv_ref[...] = proj("gv").astype(BF16)
    silu = lambda t: t * jax.nn.sigmoid(t)
    gg_ref[...] = silu(proj("gg")).astype(BF16)
    per_tile(proj("rq"), rq_ref, lambda t: _rope(t, cos, sa, sb))
    per_tile(proj("rk"), rk_ref, lambda t: _rope(t, cos, sa, sb) * (RET_DK ** -0.5))
    rv_ref[...] = proj("rv").astype(BF16)
    rg_ref[...] = silu(proj("rg")).astype(BF16)
    z = proj("ga").astype(BF16)
    pre = _dot(z, wg_ref[...]) + bg_ref[...]
    la = (jnp.minimum(pre, 0.0) - jnp.log1p(jnp.exp(-jnp.abs(pre)))) * (1.0 / GLA_GATE_TAU)
    nk = GLA_HEADS * GLA_DK
    laf_ref[...] = la[:, :nk]
    lab_ref[...] = la[:, nk:]


def _proj(x2d, seq, gmix, w_in, wg, bg, qg, kg, cos, sa, sb):
    t = x2d.shape[0]
    tm = min(512, seq)
    nblk = seq // tm
    row = lambda w, dt: (pl.BlockSpec((tm, w), lambda i: (i, 0)), jax.ShapeDtypeStruct((t, w), dt))
    outs = [row(_W_AQ, BF16), row(_W_AK, BF16), row(_W_AV + LANES, BF16), row(_W_GQ, BF16), row(_W_GK, BF16),
            row(_W_GV, BF16), row(_W_GG, BF16), row(_W_RQ, BF16), row(_W_RK, BF16), row(_W_RV, BF16),
            row(_W_RG, BF16), row(GLA_HEADS * GLA_DK, F32), row(GLA_HEADS * GLA_DK, F32)]
    tab = pl.BlockSpec((tm, LANES), lambda i: (i % nblk, 0))
    return pl.pallas_call(
        _proj_kernel,
        grid=(t // tm,),
        in_specs=[pl.BlockSpec((tm, D_MODEL), lambda i: (i, 0)), _const_spec((1, D_MODEL)),
                  _const_spec(w_in.shape), _const_spec(wg.shape), _const_spec(bg.shape),
                  _const_spec((1, LANES)), _const_spec((1, LANES)), tab, tab, tab],
        out_specs=[o[0] for o in outs],
        out_shape=[o[1] for o in outs],
        compiler_params=pltpu.CompilerParams(dimension_semantics=("parallel",), vmem_limit_bytes=VMEM_LIMIT),
        name="proj",
    )(x2d, gmix, w_in, wg, bg, qg, kg, cos, sa, sb)


def _stack_query_heads(q_ref, tq):
    lo_half = lax.broadcasted_iota(jnp.int32, (tq, LANES), 1) < HEAD_DIM
    qs = []
    for h in range(ATT_HEADS):
        blk = q_ref[:, (h // 2) * LANES:(h // 2 + 1) * LANES].astype(F32)
        src_half, dst_half = h % 2, h // ATT_GROUPS
        if src_half != dst_half:
            blk = pltpu.roll(blk, HEAD_DIM, 1)
        keep = lo_half if dst_half == 0 else jnp.logical_not(lo_half)
        qs.append(jnp.where(keep, blk, 0.0).astype(BF16))
    return jnp.concatenate(qs, axis=0)


def _store_heads(acc_ref, o_ref, tq):
    lo_half = lax.broadcasted_iota(jnp.int32, (tq, LANES), 1) < HEAD_DIM
    out = acc_ref[:, :LANES] / acc_ref[:, LANES:]
    for j in range(ATT_HEADS // 2):
        a = out[(2 * j) * tq:(2 * j + 1) * tq]
        b = out[(2 * j + 1) * tq:(2 * j + 2) * tq]
        if (2 * j) // ATT_GROUPS == 0:
            b = pltpu.roll(b, HEAD_DIM, 1)
        else:
            a = pltpu.roll(a, HEAD_DIM, 1)
        o_ref[:, j * LANES:(j + 1) * LANES] = jnp.where(lo_half, a, b).astype(o_ref.dtype)


def _attn_shifted_kernel(shift_ref, q_ref, k_ref, v_ref, o_ref, acc_ref, *, tq, tk, seq):
    q = _stack_query_heads(q_ref, tq)
    shift = shift_ref[0]
    acc_ref[...] = jnp.zeros(acc_ref.shape, F32)

    def body(c, carry):
        start = pl.multiple_of(c * tk, tk)
        p = jnp.exp2(_dot_nt(q, k_ref[pl.ds(start, tk), :]) - shift).astype(BF16)
        acc_ref[...] += _dot(p, v_ref[pl.ds(start, tk), :])
        return carry

    lax.fori_loop(0, seq // tk, body, 0, unroll=4)
    _store_heads(acc_ref, o_ref, tq)


def _attn_online_kernel(q_ref, k_ref, v_ref, o_ref, acc_ref, m_ref, *, tq, tk, seq):
    q = _stack_query_heads(q_ref, tq)
    m_ref[...] = jnp.full(m_ref.shape, -jnp.inf, F32)
    acc_ref[...] = jnp.zeros(acc_ref.shape, F32)

    def body(c, carry):
        start = pl.multiple_of(c * tk, tk)
        s = _dot_nt(q, k_ref[pl.ds(start, tk), :])
        m_old = m_ref[...]
        m_new = jnp.maximum(m_old, jnp.max(s, axis=-1, keepdims=True))
        p = jnp.exp2(s - m_new).astype(BF16)
        acc_ref[...] = jnp.exp2(m_old - m_new) * acc_ref[...] + _dot(p, v_ref[pl.ds(start, tk), :])
        m_ref[...] = m_new
        return carry

    lax.fori_loop(0, seq // tk, body, 0)
    _store_heads(acc_ref, o_ref, tq)


def _attention(q, k, v, shift, batch, seq):
    t = q.shape[0]
    tq = 128
    tk = min(512, seq)
    nq = seq // tq
    m_rows = ATT_HEADS * tq
    row_spec = pl.BlockSpec((tq, _W_AQ), lambda b, i: (b * nq + i, 0))
    k_spec = pl.BlockSpec((seq, LANES), lambda b, i: (b, 0))
    v_spec = pl.BlockSpec((seq, 2 * LANES), lambda b, i: (b, 0))
    common = dict(
        grid=(batch, nq),
        out_specs=row_spec,
        out_shape=jax.ShapeDtypeStruct((t, _W_AQ), BF16),
        compiler_params=pltpu.CompilerParams(dimension_semantics=("parallel", "parallel"),
                                             vmem_limit_bytes=VMEM_LIMIT))
    acc = pltpu.VMEM((m_rows, 2 * LANES), F32)

    def shifted(q, k, v):
        return pl.pallas_call(
            functools.partial(_attn_shifted_kernel, tq=tq, tk=tk, seq=seq),
            in_specs=[pl.BlockSpec(memory_space=pltpu.SMEM), row_spec, k_spec, v_spec],
            scratch_shapes=[acc], name="attention", **common)(shift, q, k, v)

    def online(q, k, v):
        return pl.pallas_call(
            functools.partial(_attn_online_kernel, tq=tq, tk=tk, seq=seq),
            in_specs=[row_spec, k_spec, v_spec],
            scratch_shapes=[acc, pltpu.VMEM((m_rows, 1), F32)], name="attention_online", **common)(q, k, v)

    return lax.cond(shift[0] <= ATT_MAX_SHIFT, shifted, online, q, k, v)


def _head_rows(x):
    lane_head = lax.broadcasted_iota(jnp.int32, x.shape, 1) // GLA_DK
    zero = jnp.zeros_like(x)
    return jnp.concatenate([jnp.where(lane_head == 0, x, zero), jnp.where(lane_head == 1, x, zero)], axis=0)


def _block_diag_values(v):
    first = lax.broadcasted_iota(jnp.int32, v.shape, 1) < GLA_DV
    zero = jnp.zeros_like(v)
    return jnp.concatenate([jnp.where(first, v, zero), jnp.where(first, zero, v)], axis=0)


def _scan_rows(n, unroll, nchunk, c, reverse):
    rows = []
    for u in range(unroll):
        idx = n * unroll + u
        ch = (nchunk - 1 - idx) if reverse else idx
        rows.append(pl.ds(pl.multiple_of(ch * c, c), c))
    return rows


def _gla_kernel(*refs, ts, reverse):
    if reverse:
        q_ref, k_ref, v_ref, la_ref, of_ref, o_ref, st_ref = refs
    else:
        q_ref, k_ref, v_ref, la_ref, o_ref, st_ref = refs
    c = GLA_CHUNK
    nchunk = ts // c

    @pl.when(pl.program_id(2) == 0)
    def _():
        st_ref[...] = jnp.zeros(st_ref.shape, F32)

    ri = lax.broadcasted_iota(jnp.int32, (c, 2 * c), 0)
    ci = lax.broadcasted_iota(jnp.int32, (c, 2 * c), 1) % c
    causal2 = (ri <= ci) if reverse else (ri >= ci)
    cum = causal2[:, :c].astype(BF16)
    mid = c // 2 if reverse else c // 2 - 1
    last = 0 if reverse else c - 1

    def local_parts(rows_list):
        cums = [_dot(cum, jnp.concatenate(_split2(la_ref[rows, :]), axis=1)) for rows in rows_list]
        scores, tails = [], []
        for rows, parts in zip(rows_list, cums):
            qf = q_ref[rows, :].astype(F32)
            kf = k_ref[rows, :].astype(F32)
            b = parts[:, :LANES] + parts[:, LANES:]
            b_ref = b[mid:mid + 1, :]
            b_last = b[last:last + 1, :]
            q_in = (qf * jnp.exp(b - b_ref)).astype(BF16)
            k_in = (kf * jnp.exp(b_ref - b)).astype(BF16)
            scores.append(_dot_nt(q_in, _head_rows(k_in)))
            kd = _head_rows((kf * jnp.exp(b_last - b)).astype(BF16))
            decay = jnp.transpose(jnp.broadcast_to(jnp.exp(b_last), (LANES, LANES)))
            tails.append(((qf * jnp.exp(b)).astype(BF16), jnp.concatenate([decay, decay], axis=1), kd))
        out = []
        for rows, s, (q_dec, decay, kd) in zip(rows_list, scores, tails):
            vbd = _block_diag_values(v_ref[rows, :])
            intra = _dot(jnp.where(causal2, s, 0.0).astype(BF16), vbd)
            out.append((intra, q_dec, decay, _dot_tn(kd, vbd)))
        return out

    unroll = min(SCAN_UNROLL, nchunk)

    def body(n, carry):
        rows = _scan_rows(n, unroll, nchunk, c, reverse)
        st = st_ref[...]
        for r, (intra, q_dec, decay, upd) in zip(rows, local_parts(rows)):
            o = intra + _dot(q_dec, st.astype(BF16))
            o_ref[r, :] = (o + of_ref[r, :]) if reverse else o
            st = st * decay + upd
        st_ref[...] = st
        return carry

    lax.fori_loop(0, nchunk // unroll, body, 0)


def _scan_specs(batch, seq, ts, reverse):
    ns = seq // ts
    blk = (lambda i: ns - 1 - i) if reverse else (lambda i: i)
    narrow = pl.BlockSpec((ts, LANES), lambda b, p, i: (b * ns + blk(i), p))
    wide = pl.BlockSpec((ts, 2 * GLA_DV), lambda b, p, i: (b * ns + blk(i), p))
    return ns, narrow, wide


def _gla(q, k, v, la_f, la_b, batch, seq):
    t = q.shape[0]
    ts = min(1024, seq)
    params = pltpu.CompilerParams(dimension_semantics=("parallel", "parallel", "arbitrary"),
                                  vmem_limit_bytes=VMEM_LIMIT)
    scratch = [pltpu.VMEM((LANES, 2 * GLA_DV), F32)]
    out_shape = jax.ShapeDtypeStruct((t, GLA_HEADS * GLA_DV), F32)
    ns, narrow, wide = _scan_specs(batch, seq, ts, False)
    o_f = pl.pallas_call(
        functools.partial(_gla_kernel, ts=ts, reverse=False),
        grid=(batch, GLA_HEADS // 2, ns),
        in_specs=[narrow, narrow, wide, narrow],
        out_specs=wide, out_shape=out_shape,
        scratch_shapes=scratch, compiler_params=params, name="gla_fwd",
    )(q, k, v, la_f)
    ns, narrow, wide = _scan_specs(batch, seq, ts, True)
    return pl.pallas_call(
        functools.partial(_gla_kernel, ts=ts, reverse=True),
        grid=(batch, GLA_HEADS // 2, ns),
        in_specs=[narrow, narrow, wide, narrow, wide],
        out_specs=wide, out_shape=out_shape,
        scratch_shapes=scratch, compiler_params=params, name="gla_bwd",
    )(q, k, v, la_b, o_f)


def _ret_kernel(*refs, ts, reverse):
    if reverse:
        q_ref, k_ref, v_ref, dm_ref, qd_ref, kd_ref, cd_ref, of_ref, o_ref, st_ref = refs
    else:
        q_ref, k_ref, v_ref, dm_ref, qd_ref, kd_ref, cd_ref, o_ref, st_ref = refs
    c = RET_CHUNK
    nchunk = ts // c

    @pl.when(pl.program_id(2) == 0)
    def _():
        st_ref[...] = jnp.zeros(st_ref.shape, F32)

    def local_parts(rows_list):
        scores = [_dot_nt(q_ref[rows, :], _head_rows(k_ref[rows, :])) for rows in rows_list]
        out = []
        for rows, s in zip(rows_list, scores):
            vbd = _block_diag_values(v_ref[rows, :])
            intra = _dot((s * dm_ref[0]).astype(BF16), vbd)
            kd = (jnp.concatenate([k_ref[rows, :], k_ref[rows, :]], axis=0).astype(F32) * kd_ref[0]).astype(BF16)
            out.append((intra, q_ref[rows, :], _dot_tn(kd, vbd)))
        return out

    unroll = min(SCAN_UNROLL, nchunk)

    def body(n, carry):
        rows = _scan_rows(n, unroll, nchunk, c, reverse)
        st = st_ref[...]
        for r, (intra, qc, upd) in zip(rows, local_parts(rows)):
            o = intra + _dot(qc, st.astype(BF16)) * qd_ref[0]
            o_ref[r, :] = (o + of_ref[r, :]) if reverse else o
            st = st * cd_ref[0] + upd
        st_ref[...] = st
        return carry

    lax.fori_loop(0, nchunk // unroll, body, 0)


def _ret_tables(reverse):
    c = RET_CHUNK
    lg = jnp.log(1.0 - 2.0 ** (-5.0 - jnp.arange(RET_HEADS, dtype=F32)))
    idx = jnp.arange(c, dtype=F32)
    if reverse:
        lg = lg[::-1]
        diff = idx[None, :] - idx[:, None]
        q_pow, k_pow = c - idx, idx
    else:
        diff = idx[:, None] - idx[None, :]
        q_pow, k_pow = idx + 1.0, c - 1.0 - idx
    lgh = lg[:, None, None]
    dm = jnp.where(diff >= 0, jnp.exp(lgh * jnp.maximum(diff, 0.0)), 0.0)
    qd = jnp.exp(lg[:, None] * q_pow[None, :])
    kd = jnp.exp(lg[:, None] * k_pow[None, :])
    cd = jnp.exp(lg * c)
    npair = RET_HEADS // 2
    dm = dm.reshape(npair, 2, c, c).transpose(0, 2, 1, 3).reshape(npair, c, 2 * c)
    qd = jnp.repeat(qd.reshape(npair, 2, c).transpose(0, 2, 1), RET_DV, axis=2)
    own = jnp.arange(LANES)[None, :] // RET_DK == jnp.arange(2)[:, None]
    kd = jnp.where(own[None, :, None, :], kd.reshape(npair, 2, c, 1), 0.0).reshape(npair, 2 * c, LANES)
    cd = jnp.broadcast_to(jnp.repeat(cd.reshape(npair, 2), RET_DK, axis=1)[:, :, None], (npair, LANES, 2 * RET_DV))
    return dm, qd, kd, cd


def _retention(q, k, v, batch, seq):
    t = q.shape[0]
    ts = min(1024, seq)
    c = RET_CHUNK
    params = pltpu.CompilerParams(dimension_semantics=("parallel", "parallel", "arbitrary"),
                                  vmem_limit_bytes=VMEM_LIMIT)
    scratch = [pltpu.VMEM((LANES, 2 * RET_DV), F32)]
    out_shape = jax.ShapeDtypeStruct((t, RET_HEADS * RET_DV), F32)
    tab_specs = [pl.BlockSpec((1, c, 2 * c), lambda b, p, i: (p, 0, 0)),
                 pl.BlockSpec((1, c, 2 * RET_DV), lambda b, p, i: (p, 0, 0)),
                 pl.BlockSpec((1, 2 * c, LANES), lambda b, p, i: (p, 0, 0)),
                 pl.BlockSpec((1, LANES, 2 * RET_DV), lambda b, p, i: (p, 0, 0))]
    ns, narrow, wide = _scan_specs(batch, seq, ts, False)
    o_f = pl.pallas_call(
        functools.partial(_ret_kernel, ts=ts, reverse=False),
        grid=(batch, RET_HEADS // 2, ns),
        in_specs=[narrow, narrow, wide] + tab_specs,
        out_specs=wide, out_shape=out_shape,
        scratch_shapes=scratch, compiler_params=params, name="ret_fwd",
    )(q, k, v, *_ret_tables(False))
    ns, narrow, wide = _scan_specs(batch, seq, ts, True)
    return pl.pallas_call(
        functools.partial(_ret_kernel, ts=ts, reverse=True),
        grid=(batch, RET_HEADS // 2, ns),
        in_specs=[narrow, narrow, wide] + tab_specs + [wide],
        out_specs=wide, out_shape=out_shape,
        scratch_shapes=scratch, compiler_params=params, name="ret_bwd",
    )(q, k, v, *_ret_tables(True), o_f)


def _merge_mlp_kernel(x_ref, oa_ref, og_ref, gg_ref, or_ref, rg_ref, gmix_ref, gmlp_ref, gfin_ref, ggain_ref,
                      wm_ref, wb_ref, wo_ref, wu_ref, wd_ref, y_ref, *, final):
    x = x_ref[...]
    xn = _rms(x, gmix_ref[...]).astype(BF16)

    def per_head(o_ref, gate_ref, norm):
        heads = []
        for h in range(BRANCH_W // LANES):
            sl = slice(h * LANES, (h + 1) * LANES)
            heads.append((norm(o_ref[:, sl]) * gate_ref[:, sl].astype(F32)).astype(BF16))
        return jnp.concatenate(heads, axis=1)

    def group_norm(o):
        xc = o - jnp.mean(o, axis=-1, keepdims=True)
        return xc * lax.rsqrt(jnp.mean(xc * xc, axis=-1, keepdims=True) + EPS)

    branches = (oa_ref[...],
                per_head(og_ref, gg_ref, lambda o: _rms(o, ggain_ref[...])),
                per_head(or_ref, rg_ref, group_norm))
    merged = None
    for i, br in enumerate(branches):
        gate = jax.nn.sigmoid(_dot(xn, wm_ref[:, i * D_MODEL:(i + 1) * D_MODEL]))
        term = gate * _dot(br, wb_ref[i])
        merged = term if merged is None else merged + term
    h = x + _dot(merged.astype(BF16), wo_ref[...])
    hn = _rms(h, gmlp_ref[...]).astype(BF16)
    y = h
    for j in range(D_FF // D_MODEL):
        u = jnp.maximum(_dot(hn, wu_ref[:, j * D_MODEL:(j + 1) * D_MODEL]), 0.0)
        y = y + _dot((u * u).astype(BF16), wd_ref[j * D_MODEL:(j + 1) * D_MODEL, :])
    if final:
        y = _rms(y, gfin_ref[...])
    y_ref[...] = y


def _merge_mlp(x2d, oa, og, gg, o_r, rg, gmix, gmlp, gfin, ggain, wm, wb, wo, wu, wd, final):
    t = x2d.shape[0]
    tm = min(512, t)
    row = lambda w: pl.BlockSpec((tm, w), lambda i: (i, 0))
    vec = _const_spec((1, D_MODEL))
    return pl.pallas_call(
        functools.partial(_merge_mlp_kernel, final=final),
        grid=(t // tm,),
        in_specs=[row(D_MODEL)] + [row(BRANCH_W)] * 5 + [vec, vec, vec, _const_spec((1, GLA_DV)),
                  _const_spec(wm.shape), _const_spec(wb.shape), _const_spec(wo.shape),
                  _const_spec(wu.shape), _const_spec(wd.shape)],
        out_specs=row(D_MODEL),
        out_shape=jax.ShapeDtypeStruct((t, D_MODEL), F32),
        compiler_params=pltpu.CompilerParams(dimension_semantics=("parallel",), vmem_limit_bytes=VMEM_LIMIT),
        name="merge_mlp",
    )(x2d, oa, og, gg, o_r, rg, gmix, gmlp, gfin, ggain, wm, wb, wo, wu, wd)


def _rope_tables(seq):
    pos = jnp.arange(seq)
    axis_dim = HEAD_DIM // 2
    inv = ROPE_THETA ** (-jnp.arange(0, axis_dim, 2, dtype=F32) / axis_dim)
    ang_r = (pos // GRID_W).astype(F32)[:, None] * inv[None, :]
    ang_c = (pos % GRID_W).astype(F32)[:, None] * inv[None, :]
    ang = jnp.concatenate([ang_r, ang_r, ang_c, ang_c], axis=-1)
    ang = jnp.concatenate([ang, ang], axis=-1)
    first = (jnp.arange(LANES) % axis_dim) < axis_dim // 2
    cos, sin = jnp.cos(ang), jnp.sin(ang)
    return cos, jnp.where(first, -sin, 0.0), jnp.where(first, 0.0, sin)


def _prep_layer(w_in, gla_w_gate, gla_b_gate, q_gain, k_gain, gla_gain, w_branch, w_merge, w_out, w_up, w_down):
    splits = (512, 128, 128, 256, 256, 512, 32, 512, 256, 256, 512, 512)
    names = ("aq", "ak", "av", "gq", "gk", "gv", "ga", "gg", "rq", "rk", "rv", "rg")
    cols, start = {}, 0
    for n, w in zip(names, splits):
        cols[n] = w_in[:, start:start + w]
        start += w
    cols["ga"] = jnp.pad(cols["ga"], ((0, 0), (0, _W_GA - 2 * GLA_GATE_RANK)))
    order = sorted(_OFF, key=lambda n: _OFF[n][0])
    w_in_r = jnp.concatenate([cols[n] for n in order], axis=1).astype(BF16)
    nk = GLA_HEADS * GLA_DK
    r = GLA_GATE_RANK
    wg = jnp.zeros((_W_GA, 2 * nk), F32)
    wg = wg.at[:r, :nk].set(gla_w_gate[0]).at[r:2 * r, nk:].set(gla_w_gate[1]).astype(BF16)
    bg = jnp.concatenate([gla_b_gate[0], gla_b_gate[1]])[None, :]
    two = lambda g: jnp.concatenate([g, g])[None, :]
    att_shift = jnp.ceil(1.02 * ATT_Q_SCALE * HEAD_DIM * jnp.max(jnp.abs(q_gain)) * jnp.max(jnp.abs(k_gain)))
    return dict(w_in=w_in_r, wg=wg, bg=bg, qg=two(q_gain), kg=two(k_gain), gla_gain=gla_gain[None, :],
                att_shift=att_shift.reshape(1),
                wb=w_branch.astype(BF16), wm=w_merge.astype(BF16), wo=w_out.astype(BF16),
                wu=w_up.astype(BF16), wd=w_down.astype(BF16))


def _trunk(x, layers, norm_mix, norm_mlp, norm_final):
    batch, seq, _ = x.shape
    x2d = x.reshape(batch * seq, D_MODEL)
    cos, sa, sb = _rope_tables(seq)
    depth = len(layers)
    for l, w in enumerate(layers):
        gmix = norm_mix[l][None, :]
        (aq, ak, av, gq, gk, gv, gg, rq, rk, rv, rg, la_f, la_b) = _proj(
            x2d, seq, gmix, w["w_in"], w["wg"], w["bg"], w["qg"], w["kg"], cos, sa, sb)
        o_att = _attention(aq, ak, av, w["att_shift"], batch, seq)
        o_gla = _gla(gq, gk, gv, la_f, la_b, batch, seq)
        o_ret = _retention(rq, rk, rv, batch, seq)
        x2d = _merge_mlp(x2d, o_att, o_gla, gg, o_ret, rg, gmix, norm_mlp[l][None, :], norm_final[None, :],
                         w["gla_gain"], w["wm"], w["wb"], w["wo"], w["wu"], w["wd"], final=(l == depth - 1))
    return x2d.reshape(batch, seq, D_MODEL)


def kernel(x_prompt, x_sample, norm_mix, norm_mlp, w_in, attn_q_norm, attn_k_norm, gla_w_gate, gla_b_gate,
           gla_out_norm, w_branch, w_merge, w_out, w_up, w_down, norm_final):
    layers = [_prep_layer(w_in[l], gla_w_gate[l], gla_b_gate[l], attn_q_norm[l], attn_k_norm[l],
                          gla_out_norm[l], w_branch[l], w_merge[l], w_out[l], w_up[l], w_down[l])
              for l in range(w_in.shape[0])]
    y_prompt = _trunk(x_prompt, layers, norm_mix, norm_mlp, norm_final)
    y_sample = _trunk(x_sample, layers, norm_mix, norm_mlp, norm_final)
    return (y_prompt, y_sample)
```

```python
import functools
import math

import jax
import jax.numpy as jnp
from jax import lax
from jax.experimental import pallas as pl
from jax.experimental.pallas import tpu as pltpu

F32 = jnp.float32
BF16 = jnp.bfloat16

D_MODEL = 1024
GRID_W = 64
HEAD_DIM = 64
ATT_HEADS = 8
ATT_KV_HEADS = 2
ROPE_THETA = 10000.0
GLA_HEADS = 4
GLA_DK = 64
GLA_DV = 128
GLA_GATE_RANK = 16
GLA_GATE_TAU = 16.0
GLA_CHUNK = 64
RET_HEADS = 4
RET_DK = 64
RET_DV = 128
RET_CHUNK = 128
BRANCH_W = 512
N_BRANCH = 3
D_FF = 4 * D_MODEL
EPS = 1e-6

ATT_GROUPS = ATT_HEADS // ATT_KV_HEADS
ATT_Q_SCALE = HEAD_DIM ** -0.5 * math.log2(math.e)
ATT_MAX_SHIFT = 40.0
SCAN_UNROLL = 16

LANES = 128
VMEM_LIMIT = 56 * 1024 * 1024

TOKEN_TILE = 512
ATT_Q_TILE = 512
ATT_K_TILE = 512
SCAN_TILE = 2048

_W_AQ, _W_AK, _W_AV = 512, 128, 128
_W_GQ, _W_GK, _W_GV, _W_GG = 256, 256, 512, 512
_W_RQ, _W_RK, _W_RV, _W_RG = 256, 256, 512, 512
_W_GA = LANES
_OFF = {}
_o = 0
for _n, _w in (("aq", _W_AQ), ("ak", _W_AK), ("av", _W_AV), ("gq", _W_GQ), ("gk", _W_GK), ("gv", _W_GV),
               ("gg", _W_GG), ("rq", _W_RQ), ("rk", _W_RK), ("rv", _W_RV), ("rg", _W_RG), ("ga", _W_GA)):
    _OFF[_n] = (_o, _w)
    _o += _w
IN_WIDTH_PADDED = _o


def _const_spec(shape):
    nd = len(shape)
    return pl.BlockSpec(shape, lambda *_: (0,) * nd, pipeline_mode=pl.Buffered(1))


def _dot(a, b):
    return jnp.dot(a, b, preferred_element_type=F32)


def _dot_nt(a, b):
    return lax.dot_general(a, b, (((1,), (1,)), ((), ())), preferred_element_type=F32)


def _dot_tn(a, b):
    return lax.dot_general(a, b, (((0,), (0,)), ((), ())), preferred_element_type=F32)


def _split2(x):
    hi = x.astype(BF16)
    return hi, (x - hi.astype(F32)).astype(BF16)


def _rms(x, gain):
    return x * lax.rsqrt(jnp.mean(x * x, axis=-1, keepdims=True) + EPS) * gain


def _rope(x, cos, sin_a, sin_b):
    return x * cos + pltpu.roll(x, LANES - 16, 1) * sin_a + pltpu.roll(x, 16, 1) * sin_b


def _proj_kernel(x_ref, gmix_ref, w_ref, wg_ref, bg_ref, qg_ref, kg_ref, cos_ref, sa_ref, sb_ref,
                 aq_ref, ak_ref, av_ref, gq_ref, gk_ref, gv_ref, gg_ref, rq_ref, rk_ref, rv_ref, rg_ref,
                 laf_ref, lab_ref):
    xn = _rms(x_ref[...], gmix_ref[...]).astype(BF16)
    cos, sa, sb = cos_ref[...], sa_ref[...], sb_ref[...]

    def proj(name):
        off, w = _OFF[name]
        return _dot(xn, w_ref[:, off:off + w])

    lo_half = lax.broadcasted_iota(jnp.int32, (x_ref.shape[0], LANES), 1) < HEAD_DIM

    def head_norm(t, gain):
        y = t * t
        ss_lo = jnp.sum(jnp.where(lo_half, y, 0.0), axis=-1, keepdims=True)
        ss_hi = jnp.sum(jnp.where(lo_half, 0.0, y), axis=-1, keepdims=True)
        ss = jnp.where(lo_half, ss_lo, ss_hi)
        return t * lax.rsqrt(ss * (1.0 / HEAD_DIM) + EPS) * gain

    def per_tile(p, out_ref, fn):
        for j in range(p.shape[1] // LANES):
            out_ref[:, j * LANES:(j + 1) * LANES] = fn(p[:, j * LANES:(j + 1) * LANES]).astype(out_ref.dtype)

    per_tile(proj("aq"), aq_ref, lambda t: _rope(head_norm(t, qg_ref[...]), cos, sa, sb) * ATT_Q_SCALE)
    akv = _dot(xn, w_ref[:, _OFF["ak"][0]:_OFF["av"][0] + _W_AV])
    ak_ref[...] = _rope(head_norm(akv[:, :_W_AK], kg_ref[...]), cos, sa, sb).astype(BF16)
    av_ref[:, :_W_AV] = akv[:, _W_AK:].astype(BF16)
    av_ref[:, _W_AV:] = jnp.ones((av_ref.shape[0], LANES), BF16)
    gq_ref[...] = (proj("gq") * (GLA_DK ** -0.5)).astype(BF16)
    gk_ref[...] = proj("gk").astype(BF16)
    gv_ref[...] = proj("gv").astype(BF16)
    silu = lambda t: t * jax.nn.sigmoid(t)
    gg_ref[...] = silu(proj("gg")).astype(BF16)
    per_tile(proj("rq"), rq_ref, lambda t: _rope(t, cos, sa, sb))
    per_tile(proj("rk"), rk_ref, lambda t: _rope(t, cos, sa, sb) * (RET_DK ** -0.5))
    rv_ref[...] = proj("rv").astype(BF16)
    rg_ref[...] = silu(proj("rg")).astype(BF16)
    z = proj("ga").astype(BF16)
    pre = _dot(z, wg_ref[...]) + bg_ref[...]
    la = (jnp.minimum(pre, 0.0) - jnp.log1p(jnp.exp(-jnp.abs(pre)))) * (1.0 / GLA_GATE_TAU)
    nk = GLA_HEADS * GLA_DK
    laf_ref[...] = la[:, :nk]
    lab_ref[...] = la[:, nk:]


def _proj(x2d, seq, gmix, w_in, wg, bg, qg, kg, cos, sa, sb):
    t = x2d.shape[0]
    tm = min(TOKEN_TILE, seq)
    nblk = seq // tm
    row = lambda w, dt: (pl.BlockSpec((tm, w), lambda i: (i, 0)), jax.ShapeDtypeStruct((t, w), dt))
    outs = [row(_W_AQ, BF16), row(_W_AK, BF16), row(_W_AV + LANES, BF16), row(_W_GQ, BF16), row(_W_GK, BF16),
            row(_W_GV, BF16), row(_W_GG, BF16), row(_W_RQ, BF16), row(_W_RK, BF16), row(_W_RV, BF16),
            row(_W_RG, BF16), row(GLA_HEADS * GLA_DK, F32), row(GLA_HEADS * GLA_DK, F32)]
    tab = pl.BlockSpec((tm, LANES), lambda i: (i % nblk, 0))
    return pl.pallas_call(
        _proj_kernel,
        grid=(t // tm,),
        in_specs=[pl.BlockSpec((tm, D_MODEL), lambda i: (i, 0)), _const_spec((1, D_MODEL)),
                  _const_spec(w_in.shape), _const_spec(wg.shape), _const_spec(bg.shape),
                  _const_spec((1, LANES)), _const_spec((1, LANES)), tab, tab, tab],
        out_specs=[o[0] for o in outs],
        out_shape=[o[1] for o in outs],
        compiler_params=pltpu.CompilerParams(dimension_semantics=("parallel",), vmem_limit_bytes=VMEM_LIMIT),
        name="proj",
    )(x2d, gmix, w_in, wg, bg, qg, kg, cos, sa, sb)


def _stack_query_heads(q_ref, tq):
    lo_half = lax.broadcasted_iota(jnp.int32, (tq, LANES), 1) < HEAD_DIM
    qs = []
    for h in range(ATT_HEADS):
        blk = q_ref[:, (h // 2) * LANES:(h // 2 + 1) * LANES].astype(F32)
        src_half, dst_half = h % 2, h // ATT_GROUPS
        if src_half != dst_half:
            blk = pltpu.roll(blk, HEAD_DIM, 1)
        keep = lo_half if dst_half == 0 else jnp.logical_not(lo_half)
        qs.append(jnp.where(keep, blk, 0.0).astype(BF16))
    return jnp.concatenate(qs, axis=0)


def _store_heads(acc_ref, o_ref, tq):
    lo_half = lax.broadcasted_iota(jnp.int32, (tq, LANES), 1) < HEAD_DIM
    out = acc_ref[:, :LANES] / acc_ref[:, LANES:]
    for j in range(ATT_HEADS // 2):
        a = out[(2 * j) * tq:(2 * j + 1) * tq]
        b = out[(2 * j + 1) * tq:(2 * j + 2) * tq]
        if (2 * j) // ATT_GROUPS == 0:
            b = pltpu.roll(b, HEAD_DIM, 1)
        else:
            a = pltpu.roll(a, HEAD_DIM, 1)
        o_ref[:, j * LANES:(j + 1) * LANES] = jnp.where(lo_half, a, b).astype(o_ref.dtype)


def _attn_shifted_kernel(shift_ref, q_ref, k_ref, v_ref, o_ref, acc_ref, *, tq, tk, seq):
    q = _stack_query_heads(q_ref, tq)
    shift = shift_ref[0]
    acc_ref[...] = jnp.zeros(acc_ref.shape, F32)

    def body(c, carry):
        start = pl.multiple_of(c * tk, tk)
        p = jnp.exp2(_dot_nt(q, k_ref[pl.ds(start, tk), :]) - shift).astype(BF16)
        acc_ref[...] += _dot(p, v_ref[pl.ds(start, tk), :])
        return carry

    lax.fori_loop(0, seq // tk, body, 0, unroll=4)
    _store_heads(acc_ref, o_ref, tq)


def _attn_online_kernel(q_ref, k_ref, v_ref, o_ref, acc_ref, m_ref, *, tq, tk, seq):
    q = _stack_query_heads(q_ref, tq)
    m_ref[...] = jnp.full(m_ref.shape, -jnp.inf, F32)
    acc_ref[...] = jnp.zeros(acc_ref.shape, F32)

    def body(c, carry):
        start = pl.multiple_of(c * tk, tk)
        s = _dot_nt(q, k_ref[pl.ds(start, tk), :])
        m_old = m_ref[...]
        m_new = jnp.maximum(m_old, jnp.max(s, axis=-1, keepdims=True))
        p = jnp.exp2(s - m_new).astype(BF16)
        acc_ref[...] = jnp.exp2(m_old - m_new) * acc_ref[...] + _dot(p, v_ref[pl.ds(start, tk), :])
        m_ref[...] = m_new
        return carry

    lax.fori_loop(0, seq // tk, body, 0)
    _store_heads(acc_ref, o_ref, tq)


def _attention(q, k, v, shift, batch, seq):
    t = q.shape[0]
    tq = min(ATT_Q_TILE, seq)
    tk = min(ATT_K_TILE, seq)
    nq = seq // tq
    m_rows = ATT_HEADS * tq
    row_spec = pl.BlockSpec((tq, _W_AQ), lambda b, i: (b * nq + i, 0))
    k_spec = pl.BlockSpec((seq, LANES), lambda b, i: (b, 0))
    v_spec = pl.BlockSpec((seq, 2 * LANES), lambda b, i: (b, 0))
    common = dict(
        grid=(batch, nq),
        out_specs=row_spec,
        out_shape=jax.ShapeDtypeStruct((t, _W_AQ), BF16),
        compiler_params=pltpu.CompilerParams(dimension_semantics=("parallel", "parallel"),
                                             vmem_limit_bytes=VMEM_LIMIT))
    acc = pltpu.VMEM((m_rows, 2 * LANES), F32)

    def shifted(q, k, v):
        return pl.pallas_call(
            functools.partial(_attn_shifted_kernel, tq=tq, tk=tk, seq=seq),
            in_specs=[pl.BlockSpec(memory_space=pltpu.SMEM), row_spec, k_spec, v_spec],
            scratch_shapes=[acc], name="attention", **common)(shift, q, k, v)

    def online(q, k, v):
        return pl.pallas_call(
            functools.partial(_attn_online_kernel, tq=tq, tk=tk, seq=seq),
            in_specs=[row_spec, k_spec, v_spec],
            scratch_shapes=[acc, pltpu.VMEM((m_rows, 1), F32)], name="attention_online", **common)(q, k, v)

    return lax.cond(shift[0] <= ATT_MAX_SHIFT, shifted, online, q, k, v)


def _head_rows(x):
    lane_head = lax.broadcasted_iota(jnp.int32, x.shape, 1) // GLA_DK
    zero = jnp.zeros_like(x)
    return jnp.concatenate([jnp.where(lane_head == 0, x, zero), jnp.where(lane_head == 1, x, zero)], axis=0)


def _block_diag_values(v):
    first = lax.broadcasted_iota(jnp.int32, v.shape, 1) < GLA_DV
    zero = jnp.zeros_like(v)
    return jnp.concatenate([jnp.where(first, v, zero), jnp.where(first, zero, v)], axis=0)


def _scan_rows(n, unroll, nchunk, c, reverse):
    rows = []
    for u in range(unroll):
        idx = n * unroll + u
        ch = (nchunk - 1 - idx) if reverse else idx
        rows.append(pl.ds(pl.multiple_of(ch * c, c), c))
    return rows


def _gla_kernel(*refs, ts, reverse):
    if reverse:
        q_ref, k_ref, v_ref, la_ref, of_ref, o_ref, st_ref = refs
    else:
        q_ref, k_ref, v_ref, la_ref, o_ref, st_ref = refs
    c = GLA_CHUNK
    nchunk = ts // c

    @pl.when(pl.program_id(2) == 0)
    def _():
        st_ref[...] = jnp.zeros(st_ref.shape, F32)

    ri = lax.broadcasted_iota(jnp.int32, (c, 2 * c), 0)
    ci = lax.broadcasted_iota(jnp.int32, (c, 2 * c), 1) % c
    causal2 = (ri <= ci) if reverse else (ri >= ci)
    cum = causal2[:, :c].astype(BF16)
    mid = c // 2 if reverse else c // 2 - 1
    last = 0 if reverse else c - 1

    def local_parts(rows_list):
        cums = [_dot(cum, jnp.concatenate(_split2(la_ref[rows, :]), axis=1)) for rows in rows_list]
        scores, tails = [], []
        for rows, parts in zip(rows_list, cums):
            qf = q_ref[rows, :].astype(F32)
            kf = k_ref[rows, :].astype(F32)
            b = parts[:, :LANES] + parts[:, LANES:]
            b_ref = b[mid:mid + 1, :]
            b_last = b[last:last + 1, :]
            q_in = (qf * jnp.exp(b - b_ref)).astype(BF16)
            k_in = (kf * jnp.exp(b_ref - b)).astype(BF16)
            scores.append(_dot_nt(q_in, _head_rows(k_in)))
            kd = _head_rows((kf * jnp.exp(b_last - b)).astype(BF16))
            decay = jnp.transpose(jnp.broadcast_to(jnp.exp(b_last), (LANES, LANES)))
            tails.append(((qf * jnp.exp(b)).astype(BF16), jnp.concatenate([decay, decay], axis=1), kd))
        out = []
        for rows, s, (q_dec, decay, kd) in zip(rows_list, scores, tails):
            vbd = _block_diag_values(v_ref[rows, :])
            intra = _dot(jnp.where(causal2, s, 0.0).astype(BF16), vbd)
            out.append((intra, q_dec, decay, _dot_tn(kd, vbd)))
        return out

    unroll = min(SCAN_UNROLL, nchunk)

    def body(n, carry):
        rows = _scan_rows(n, unroll, nchunk, c, reverse)
        st = st_ref[...]
        for r, (intra, q_dec, decay, upd) in zip(rows, local_parts(rows)):
            o = intra + _dot(q_dec, st.astype(BF16))
            o_ref[r, :] = (o + of_ref[r, :]) if reverse else o
            st = st * decay + upd
        st_ref[...] = st
        return carry

    lax.fori_loop(0, nchunk // unroll, body, 0)


def _scan_specs(batch, seq, ts, reverse):
    ns = seq // ts
    blk = (lambda i: ns - 1 - i) if reverse else (lambda i: i)
    narrow = pl.BlockSpec((ts, LANES), lambda b, p, i: (b * ns + blk(i), p))
    wide = pl.BlockSpec((ts, 2 * GLA_DV), lambda b, p, i: (b * ns + blk(i), p))
    return ns, narrow, wide


def _gla(q, k, v, la_f, la_b, batch, seq):
    t = q.shape[0]
    ts = min(SCAN_TILE, seq)
    params = pltpu.CompilerParams(dimension_semantics=("parallel", "parallel", "arbitrary"),
                                  vmem_limit_bytes=VMEM_LIMIT)
    scratch = [pltpu.VMEM((LANES, 2 * GLA_DV), F32)]
    out_shape = jax.ShapeDtypeStruct((t, GLA_HEADS * GLA_DV), F32)
    ns, narrow, wide = _scan_specs(batch, seq, ts, False)
    o_f = pl.pallas_call(
        functools.partial(_gla_kernel, ts=ts, reverse=False),
        grid=(batch, GLA_HEADS // 2, ns),
        in_specs=[narrow, narrow, wide, narrow],
        out_specs=wide, out_shape=out_shape,
        scratch_shapes=scratch, compiler_params=params, name="gla_fwd",
    )(q, k, v, la_f)
    ns, narrow, wide = _scan_specs(batch, seq, ts, True)
    return pl.pallas_call(
        functools.partial(_gla_kernel, ts=ts, reverse=True),
        grid=(batch, GLA_HEADS // 2, ns),
        in_specs=[narrow, narrow, wide, narrow, wide],
        out_specs=wide, out_shape=out_shape,
        scratch_shapes=scratch, compiler_params=params, name="gla_bwd",
    )(q, k, v, la_b, o_f)


def _ret_kernel(*refs, ts, reverse):
    if reverse:
        q_ref, k_ref, v_ref, dm_ref, qd_ref, kd_ref, cd_ref, of_ref, o_ref, st_ref = refs
    else:
        q_ref, k_ref, v_ref, dm_ref, qd_ref, kd_ref, cd_ref, o_ref, st_ref = refs
    c = RET_CHUNK
    nchunk = ts // c

    @pl.when(pl.program_id(2) == 0)
    def _():
        st_ref[...] = jnp.zeros(st_ref.shape, F32)

    def local_parts(rows_list):
        scores = [_dot_nt(q_ref[rows, :], _head_rows(k_ref[rows, :])) for rows in rows_list]
        out = []
        for rows, s in zip(rows_list, scores):
            vbd = _block_diag_values(v_ref[rows, :])
            intra = _dot((s * dm_ref[0]).astype(BF16), vbd)
            kd = (jnp.concatenate([k_ref[rows, :], k_ref[rows, :]], axis=0).astype(F32) * kd_ref[0]).astype(BF16)
            out.append((intra, q_ref[rows, :], _dot_tn(kd, vbd)))
        return out

    unroll = min(SCAN_UNROLL, nchunk)

    def body(n, carry):
        rows = _scan_rows(n, unroll, nchunk, c, reverse)
        st = st_ref[...]
        for r, (intra, qc, upd) in zip(rows, local_parts(rows)):
            o = intra + _dot(qc, st.astype(BF16)) * qd_ref[0]
            o_ref[r, :] = (o + of_ref[r, :]) if reverse else o
            st = st * cd_ref[0] + upd
        st_ref[...] = st
        return carry

    lax.fori_loop(0, nchunk // unroll, body, 0)


def _ret_tables(reverse):
    c = RET_CHUNK
    lg = jnp.log(1.0 - 2.0 ** (-5.0 - jnp.arange(RET_HEADS, dtype=F32)))
    idx = jnp.arange(c, dtype=F32)
    if reverse:
        lg = lg[::-1]
        diff = idx[None, :] - idx[:, None]
        q_pow, k_pow = c - idx, idx
    else:
        diff = idx[:, None] - idx[None, :]
        q_pow, k_pow = idx + 1.0, c - 1.0 - idx
    lgh = lg[:, None, None]
    dm = jnp.where(diff >= 0, jnp.exp(lgh * jnp.maximum(diff, 0.0)), 0.0)
    qd = jnp.exp(lg[:, None] * q_pow[None, :])
    kd = jnp.exp(lg[:, None] * k_pow[None, :])
    cd = jnp.exp(lg * c)
    npair = RET_HEADS // 2
    dm = dm.reshape(npair, 2, c, c).transpose(0, 2, 1, 3).reshape(npair, c, 2 * c)
    qd = jnp.repeat(qd.reshape(npair, 2, c).transpose(0, 2, 1), RET_DV, axis=2)
    own = jnp.arange(LANES)[None, :] // RET_DK == jnp.arange(2)[:, None]
    kd = jnp.where(own[None, :, None, :], kd.reshape(npair, 2, c, 1), 0.0).reshape(npair, 2 * c, LANES)
    cd = jnp.broadcast_to(jnp.repeat(cd.reshape(npair, 2), RET_DK, axis=1)[:, :, None], (npair, LANES, 2 * RET_DV))
    return dm, qd, kd, cd


def _retention(q, k, v, batch, seq):
    t = q.shape[0]
    ts = min(SCAN_TILE, seq)
    c = RET_CHUNK
    params = pltpu.CompilerParams(dimension_semantics=("parallel", "parallel", "arbitrary"),
                                  vmem_limit_bytes=VMEM_LIMIT)
    scratch = [pltpu.VMEM((LANES, 2 * RET_DV), F32)]
    out_shape = jax.ShapeDtypeStruct((t, RET_HEADS * RET_DV), F32)
    tab_specs = [pl.BlockSpec((1, c, 2 * c), lambda b, p, i: (p, 0, 0)),
                 pl.BlockSpec((1, c, 2 * RET_DV), lambda b, p, i: (p, 0, 0)),
                 pl.BlockSpec((1, 2 * c, LANES), lambda b, p, i: (p, 0, 0)),
                 pl.BlockSpec((1, LANES, 2 * RET_DV), lambda b, p, i: (p, 0, 0))]
    ns, narrow, wide = _scan_specs(batch, seq, ts, False)
    o_f = pl.pallas_call(
        functools.partial(_ret_kernel, ts=ts, reverse=False),
        grid=(batch, RET_HEADS // 2, ns),
        in_specs=[narrow, narrow, wide] + tab_specs,
        out_specs=wide, out_shape=out_shape,
        scratch_shapes=scratch, compiler_params=params, name="ret_fwd",
    )(q, k, v, *_ret_tables(False))
    ns, narrow, wide = _scan_specs(batch, seq, ts, True)
    return pl.pallas_call(
        functools.partial(_ret_kernel, ts=ts, reverse=True),
        grid=(batch, RET_HEADS // 2, ns),
        in_specs=[narrow, narrow, wide] + tab_specs + [wide],
        out_specs=wide, out_shape=out_shape,
        scratch_shapes=scratch, compiler_params=params, name="ret_bwd",
    )(q, k, v, *_ret_tables(True), o_f)


def _merge_mlp_kernel(x_ref, oa_ref, og_ref, gg_ref, or_ref, rg_ref, gmix_ref, gmlp_ref, gfin_ref, ggain_ref,
                      wm_ref, wb_ref, wo_ref, wu_ref, wd_ref, y_ref, *, final):
    x = x_ref[...]
    xn = _rms(x, gmix_ref[...]).astype(BF16)

    def per_head(o_ref, gate_ref, norm):
        heads = []
        for h in range(BRANCH_W // LANES):
            sl = slice(h * LANES, (h + 1) * LANES)
            heads.append((norm(o_ref[:, sl]) * gate_ref[:, sl].astype(F32)).astype(BF16))
        return jnp.concatenate(heads, axis=1)

    def group_norm(o):
        xc = o - jnp.mean(o, axis=-1, keepdims=True)
        return xc * lax.rsqrt(jnp.mean(xc * xc, axis=-1, keepdims=True) + EPS)

    branches = (oa_ref[...],
                per_head(og_ref, gg_ref, lambda o: _rms(o, ggain_ref[...])),
                per_head(or_ref, rg_ref, group_norm))
    merged = None
    for i, br in enumerate(branches):
        gate = jax.nn.sigmoid(_dot(xn, wm_ref[:, i * D_MODEL:(i + 1) * D_MODEL]))
        term = gate * _dot(br, wb_ref[i])
        merged = term if merged is None else merged + term
    h = x + _dot(merged.astype(BF16), wo_ref[...])
    hn = _rms(h, gmlp_ref[...]).astype(BF16)
    y = h
    for j in range(D_FF // D_MODEL):
        u = jnp.maximum(_dot(hn, wu_ref[:, j * D_MODEL:(j + 1) * D_MODEL]), 0.0)
        y = y + _dot((u * u).astype(BF16), wd_ref[j * D_MODEL:(j + 1) * D_MODEL, :])
    if final:
        y = _rms(y, gfin_ref[...])
    y_ref[...] = y


def _merge_mlp(x2d, oa, og, gg, o_r, rg, gmix, gmlp, gfin, ggain, wm, wb, wo, wu, wd, final):
    t = x2d.shape[0]
    tm = min(TOKEN_TILE, t)
    row = lambda w: pl.BlockSpec((tm, w), lambda i: (i, 0))
    vec = _const_spec((1, D_MODEL))
    return pl.pallas_call(
        functools.partial(_merge_mlp_kernel, final=final),
        grid=(t // tm,),
        in_specs=[row(D_MODEL)] + [row(BRANCH_W)] * 5 + [vec, vec, vec, _const_spec((1, GLA_DV)),
                  _const_spec(wm.shape), _const_spec(wb.shape), _const_spec(wo.shape),
                  _const_spec(wu.shape), _const_spec(wd.shape)],
        out_specs=row(D_MODEL),
        out_shape=jax.ShapeDtypeStruct((t, D_MODEL), F32),
        compiler_params=pltpu.CompilerParams(dimension_semantics=("parallel",), vmem_limit_bytes=VMEM_LIMIT),
        name="merge_mlp",
    )(x2d, oa, og, gg, o_r, rg, gmix, gmlp, gfin, ggain, wm, wb, wo, wu, wd)


def _rope_tables(seq):
    pos = jnp.arange(seq)
    axis_dim = HEAD_DIM // 2
    inv = ROPE_THETA ** (-jnp.arange(0, axis_dim, 2, dtype=F32) / axis_dim)
    ang_r = (pos // GRID_W).astype(F32)[:, None] * inv[None, :]
    ang_c = (pos % GRID_W).astype(F32)[:, None] * inv[None, :]
    ang = jnp.concatenate([ang_r, ang_r, ang_c, ang_c], axis=-1)
    ang = jnp.concatenate([ang, ang], axis=-1)
    first = (jnp.arange(LANES) % axis_dim) < axis_dim // 2
    cos, sin = jnp.cos(ang), jnp.sin(ang)
    return cos, jnp.where(first, -sin, 0.0), jnp.where(first, 0.0, sin)


def _prep_layer(w_in, gla_w_gate, gla_b_gate, q_gain, k_gain, gla_gain, w_branch, w_merge, w_out, w_up, w_down):
    splits = (512, 128, 128, 256, 256, 512, 32, 512, 256, 256, 512, 512)
    names = ("aq", "ak", "av", "gq", "gk", "gv", "ga", "gg", "rq", "rk", "rv", "rg")
    w_in = w_in.astype(BF16)
    cols, start = {}, 0
    for n, w in zip(names, splits):
        cols[n] = w_in[:, start:start + w]
        start += w
    cols["ga"] = jnp.pad(cols["ga"], ((0, 0), (0, _W_GA - 2 * GLA_GATE_RANK)))
    order = sorted(_OFF, key=lambda n: _OFF[n][0])
    w_in_r = jnp.concatenate([cols[n] for n in order], axis=1)
    nk = GLA_HEADS * GLA_DK
    r = GLA_GATE_RANK
    wg = jnp.zeros((_W_GA, 2 * nk), F32)
    wg = wg.at[:r, :nk].set(gla_w_gate[0]).at[r:2 * r, nk:].set(gla_w_gate[1]).astype(BF16)
    bg = jnp.concatenate([gla_b_gate[0], gla_b_gate[1]])[None, :]
    two = lambda g: jnp.concatenate([g, g])[None, :]
    att_shift = jnp.ceil(1.02 * ATT_Q_SCALE * HEAD_DIM * jnp.max(jnp.abs(q_gain)) * jnp.max(jnp.abs(k_gain)))
    return dict(w_in=w_in_r, wg=wg, bg=bg, qg=two(q_gain), kg=two(k_gain), gla_gain=gla_gain[None, :],
                att_shift=att_shift.reshape(1),
                wb=w_branch.astype(BF16), wm=w_merge.astype(BF16), wo=w_out.astype(BF16),
                wu=w_up.astype(BF16), wd=w_down.astype(BF16))


def _trunk(x, layers, norm_mix, norm_mlp, norm_final):
    batch, seq, _ = x.shape
    x2d = x.reshape(batch * seq, D_MODEL)
    cos, sa, sb = _rope_tables(seq)
    depth = len(layers)
    for l, w in enumerate(layers):
        gmix = norm_mix[l][None, :]
        (aq, ak, av, gq, gk, gv, gg, rq, rk, rv, rg, la_f, la_b) = _proj(
            x2d, seq, gmix, w["w_in"], w["wg"], w["bg"], w["qg"], w["kg"], cos, sa, sb)
        o_att = _attention(aq, ak, av, w["att_shift"], batch, seq)
        o_gla = _gla(gq, gk, gv, la_f, la_b, batch, seq)
        o_ret = _retention(rq, rk, rv, batch, seq)
        x2d = _merge_mlp(x2d, o_att, o_gla, gg, o_ret, rg, gmix, norm_mlp[l][None, :], norm_final[None, :],
                         w["gla_gain"], w["wm"], w["wb"], w["wo"], w["wu"], w["wd"], final=(l == depth - 1))
    return x2d.reshape(batch, seq, D_MODEL)


def kernel(x_prompt, x_sample, norm_mix, norm_mlp, w_in, attn_q_norm, attn_k_norm, gla_w_gate, gla_b_gate,
           gla_out_norm, w_branch, w_merge, w_out, w_up, w_down, norm_final):
    layers = [_prep_layer(w_in[l], gla_w_gate[l], gla_b_gate[l], attn_q_norm[l], attn_k_norm[l],
                          gla_out_norm[l], w_branch[l], w_merge[l], w_out[l], w_up[l], w_down[l])
              for l in range(w_in.shape[0])]
    y_prompt = _trunk(x_prompt, layers, norm_mix, norm_mlp, norm_final)
    y_sample = _trunk(x_sample, layers, norm_mix, norm_mlp, norm_final)
    return (y_prompt, y_sample)
```

```python
import functools
import math

import jax
import jax.numpy as jnp
from jax import lax
from jax.experimental import pallas as pl
from jax.experimental.pallas import tpu as pltpu

F32 = jnp.float32
BF16 = jnp.bfloat16

D_MODEL = 1024
GRID_W = 64
HEAD_DIM = 64
ATT_HEADS = 8
ATT_KV_HEADS = 2
ROPE_THETA = 10000.0
GLA_HEADS = 4
GLA_DK = 64
GLA_DV = 128
GLA_GATE_RANK = 16
GLA_GATE_TAU = 16.0
GLA_CHUNK = 64
RET_HEADS = 4
RET_DK = 64
RET_DV = 128
RET_CHUNK = 128
BRANCH_W = 512
N_BRANCH = 3
D_FF = 4 * D_MODEL
EPS = 1e-6

ATT_GROUPS = ATT_HEADS // ATT_KV_HEADS
ATT_Q_SCALE = HEAD_DIM ** -0.5 * math.log2(math.e)
ATT_MAX_SHIFT = 40.0
SCAN_UNROLL = 16

LANES = 128
VMEM_LIMIT = 56 * 1024 * 1024

TOKEN_TILE = 512
PROJ_TILE = 1024
ATT_Q_TILE = 512
ATT_K_TILE = 512
SCAN_TILE = 2048

_W_AQ, _W_AK, _W_AV = 512, 128, 128
_W_GQ, _W_GK, _W_GV, _W_GG = 256, 256, 512, 512
_W_RQ, _W_RK, _W_RV, _W_RG = 256, 256, 512, 512
_W_GA = LANES
_OFF = {}
_o = 0
for _n, _w in (("aq", _W_AQ), ("ak", _W_AK), ("av", _W_AV), ("gq", _W_GQ), ("gk", _W_GK), ("gv", _W_GV),
               ("gg", _W_GG), ("rq", _W_RQ), ("rk", _W_RK), ("rv", _W_RV), ("rg", _W_RG), ("ga", _W_GA)):
    _OFF[_n] = (_o, _w)
    _o += _w
IN_WIDTH_PADDED = _o


def _const_spec(shape):
    nd = len(shape)
    return pl.BlockSpec(shape, lambda *_: (0,) * nd, pipeline_mode=pl.Buffered(1))


def _dot(a, b):
    return jnp.dot(a, b, preferred_element_type=F32)


def _dot_nt(a, b):
    return lax.dot_general(a, b, (((1,), (1,)), ((), ())), preferred_element_type=F32)


def _dot_tn(a, b):
    return lax.dot_general(a, b, (((0,), (0,)), ((), ())), preferred_element_type=F32)


def _split2(x):
    hi = x.astype(BF16)
    return hi, (x - hi.astype(F32)).astype(BF16)


def _rms(x, gain):
    return x * lax.rsqrt(jnp.mean(x * x, axis=-1, keepdims=True) + EPS) * gain


def _rope(x, cos, sin_a, sin_b):
    return x * cos + pltpu.roll(x, LANES - 16, 1) * sin_a + pltpu.roll(x, 16, 1) * sin_b


def _proj_kernel(x_ref, gmix_ref, w_ref, wg_ref, bg_ref, qg_ref, kg_ref, cos_ref, sa_ref, sb_ref,
                 aq_ref, ak_ref, av_ref, gq_ref, gk_ref, gv_ref, gg_ref, rq_ref, rk_ref, rv_ref, rg_ref,
                 laf_ref, lab_ref):
    xn = _rms(x_ref[...], gmix_ref[...]).astype(BF16)
    cos, sa, sb = cos_ref[...], sa_ref[...], sb_ref[...]

    def proj(name):
        off, w = _OFF[name]
        return _dot(xn, w_ref[:, off:off + w])

    lo_half = lax.broadcasted_iota(jnp.int32, (x_ref.shape[0], LANES), 1) < HEAD_DIM

    def head_norm(t, gain):
        y = t * t
        ss_lo = jnp.sum(jnp.where(lo_half, y, 0.0), axis=-1, keepdims=True)
        ss_hi = jnp.sum(jnp.where(lo_half, 0.0, y), axis=-1, keepdims=True)
        ss = jnp.where(lo_half, ss_lo, ss_hi)
        return t * lax.rsqrt(ss * (1.0 / HEAD_DIM) + EPS) * gain

    def per_tile(p, out_ref, fn):
        for j in range(p.shape[1] // LANES):
            out_ref[:, j * LANES:(j + 1) * LANES] = fn(p[:, j * LANES:(j + 1) * LANES]).astype(out_ref.dtype)

    per_tile(proj("aq"), aq_ref, lambda t: _rope(head_norm(t, qg_ref[...]), cos, sa, sb) * ATT_Q_SCALE)
    akv = _dot(xn, w_ref[:, _OFF["ak"][0]:_OFF["av"][0] + _W_AV])
    ak_ref[...] = _rope(head_norm(akv[:, :_W_AK], kg_ref[...]), cos, sa, sb).astype(BF16)
    av_ref[:, :_W_AV] = akv[:, _W_AK:].astype(BF16)
    av_ref[:, _W_AV:] = jnp.ones((av_ref.shape[0], LANES), BF16)
    gq_ref[...] = (proj("gq") * (GLA_DK ** -0.5)).astype(BF16)
    gk_ref[...] = proj("gk").astype(BF16)
    gv_ref[...] = proj("gv").astype(BF16)
    silu = lambda t: t * jax.nn.sigmoid(t)
    gg_ref[...] = silu(proj("gg")).astype(BF16)
    per_tile(proj("rq"), rq_ref, lambda t: _rope(t, cos, sa, sb))
    per_tile(proj("rk"), rk_ref, lambda t: _rope(t, cos, sa, sb) * (RET_DK ** -0.5))
    rv_ref[...] = proj("rv").astype(BF16)
    rg_ref[...] = silu(proj("rg")).astype(BF16)
    z = proj("ga").astype(BF16)
    pre = _dot(z, wg_ref[...]) + bg_ref[...]
    la = (jnp.minimum(pre, 0.0) - jnp.log1p(jnp.exp(-jnp.abs(pre)))) * (1.0 / GLA_GATE_TAU)
    nk = GLA_HEADS * GLA_DK
    laf_ref[...] = la[:, :nk]
    lab_ref[...] = la[:, nk:]


def _proj(x2d, seq, gmix, w_in, wg, bg, qg, kg, cos, sa, sb):
    t = x2d.shape[0]
    tm = min(PROJ_TILE, seq)
    nblk = seq // tm
    row = lambda w, dt: (pl.BlockSpec((tm, w), lambda i: (i, 0)), jax.ShapeDtypeStruct((t, w), dt))
    outs = [row(_W_AQ, BF16), row(_W_AK, BF16), row(_W_AV + LANES, BF16), row(_W_GQ, BF16), row(_W_GK, BF16),
            row(_W_GV, BF16), row(_W_GG, BF16), row(_W_RQ, BF16), row(_W_RK, BF16), row(_W_RV, BF16),
            row(_W_RG, BF16), row(GLA_HEADS * GLA_DK, F32), row(GLA_HEADS * GLA_DK, F32)]
    tab = pl.BlockSpec((tm, LANES), lambda i: (i % nblk, 0))
    return pl.pallas_call(
        _proj_kernel,
        grid=(t // tm,),
        in_specs=[pl.BlockSpec((tm, D_MODEL), lambda i: (i, 0)), _const_spec((1, D_MODEL)),
                  _const_spec(w_in.shape), _const_spec(wg.shape), _const_spec(bg.shape),
                  _const_spec((1, LANES)), _const_spec((1, LANES)), tab, tab, tab],
        out_specs=[o[0] for o in outs],
        out_shape=[o[1] for o in outs],
        compiler_params=pltpu.CompilerParams(dimension_semantics=("parallel",), vmem_limit_bytes=VMEM_LIMIT),
        name="proj",
    )(x2d, gmix, w_in, wg, bg, qg, kg, cos, sa, sb)


def _stack_query_heads(q_ref, tq):
    lo_half = lax.broadcasted_iota(jnp.int32, (tq, LANES), 1) < HEAD_DIM
    qs = []
    for h in range(ATT_HEADS):
        blk = q_ref[:, (h // 2) * LANES:(h // 2 + 1) * LANES].astype(F32)
        src_half, dst_half = h % 2, h // ATT_GROUPS
        if src_half != dst_half:
            blk = pltpu.roll(blk, HEAD_DIM, 1)
        keep = lo_half if dst_half == 0 else jnp.logical_not(lo_half)
        qs.append(jnp.where(keep, blk, 0.0).astype(BF16))
    return jnp.concatenate(qs, axis=0)


def _store_heads(acc_ref, o_ref, tq):
    lo_half = lax.broadcasted_iota(jnp.int32, (tq, LANES), 1) < HEAD_DIM
    out = acc_ref[:, :LANES] / acc_ref[:, LANES:]
    for j in range(ATT_HEADS // 2):
        a = out[(2 * j) * tq:(2 * j + 1) * tq]
        b = out[(2 * j + 1) * tq:(2 * j + 2) * tq]
        if (2 * j) // ATT_GROUPS == 0:
            b = pltpu.roll(b, HEAD_DIM, 1)
        else:
            a = pltpu.roll(a, HEAD_DIM, 1)
        o_ref[:, j * LANES:(j + 1) * LANES] = jnp.where(lo_half, a, b).astype(o_ref.dtype)


def _attn_shifted_kernel(shift_ref, q_ref, k_ref, v_ref, o_ref, acc_ref, *, tq, tk, seq):
    q = _stack_query_heads(q_ref, tq)
    shift = shift_ref[0]
    acc_ref[...] = jnp.zeros(acc_ref.shape, F32)

    def body(c, carry):
        start = pl.multiple_of(c * tk, tk)
        p = jnp.exp2(_dot_nt(q, k_ref[pl.ds(start, tk), :]) - shift).astype(BF16)
        acc_ref[...] += _dot(p, v_ref[pl.ds(start, tk), :])
        return carry

    lax.fori_loop(0, seq // tk, body, 0, unroll=4)
    _store_heads(acc_ref, o_ref, tq)


def _attn_online_kernel(q_ref, k_ref, v_ref, o_ref, acc_ref, m_ref, *, tq, tk, seq):
    q = _stack_query_heads(q_ref, tq)
    m_ref[...] = jnp.full(m_ref.shape, -jnp.inf, F32)
    acc_ref[...] = jnp.zeros(acc_ref.shape, F32)

    def body(c, carry):
        start = pl.multiple_of(c * tk, tk)
        s = _dot_nt(q, k_ref[pl.ds(start, tk), :])
        m_old = m_ref[...]
        m_new = jnp.maximum(m_old, jnp.max(s, axis=-1, keepdims=True))
        p = jnp.exp2(s - m_new).astype(BF16)
        acc_ref[...] = jnp.exp2(m_old - m_new) * acc_ref[...] + _dot(p, v_ref[pl.ds(start, tk), :])
        m_ref[...] = m_new
        return carry

    lax.fori_loop(0, seq // tk, body, 0)
    _store_heads(acc_ref, o_ref, tq)


def _attention(q, k, v, shift, batch, seq):
    t = q.shape[0]
    tq = min(ATT_Q_TILE, seq)
    tk = min(ATT_K_TILE, seq)
    nq = seq // tq
    m_rows = ATT_HEADS * tq
    row_spec = pl.BlockSpec((tq, _W_AQ), lambda b, i: (b * nq + i, 0))
    k_spec = pl.BlockSpec((seq, LANES), lambda b, i: (b, 0))
    v_spec = pl.BlockSpec((seq, 2 * LANES), lambda b, i: (b, 0))
    common = dict(
        grid=(batch, nq),
        out_specs=row_spec,
        out_shape=jax.ShapeDtypeStruct((t, _W_AQ), BF16),
        compiler_params=pltpu.CompilerParams(dimension_semantics=("parallel", "parallel"),
                                             vmem_limit_bytes=VMEM_LIMIT))
    acc = pltpu.VMEM((m_rows, 2 * LANES), F32)

    def shifted(q, k, v):
        return pl.pallas_call(
            functools.partial(_attn_shifted_kernel, tq=tq, tk=tk, seq=seq),
            in_specs=[pl.BlockSpec(memory_space=pltpu.SMEM), row_spec, k_spec, v_spec],
            scratch_shapes=[acc], name="attention", **common)(shift, q, k, v)

    def online(q, k, v):
        return pl.pallas_call(
            functools.partial(_attn_online_kernel, tq=tq, tk=tk, seq=seq),
            in_specs=[row_spec, k_spec, v_spec],
            scratch_shapes=[acc, pltpu.VMEM((m_rows, 1), F32)], name="attention_online", **common)(q, k, v)

    return lax.cond(shift[0] <= ATT_MAX_SHIFT, shifted, online, q, k, v)


def _head_rows(x):
    lane_head = lax.broadcasted_iota(jnp.int32, x.shape, 1) // GLA_DK
    zero = jnp.zeros_like(x)
    return jnp.concatenate([jnp.where(lane_head == 0, x, zero), jnp.where(lane_head == 1, x, zero)], axis=0)


def _block_diag_values(v):
    first = lax.broadcasted_iota(jnp.int32, v.shape, 1) < GLA_DV
    zero = jnp.zeros_like(v)
    return jnp.concatenate([jnp.where(first, v, zero), jnp.where(first, zero, v)], axis=0)


def _scan_rows(n, unroll, nchunk, c, reverse):
    rows = []
    for u in range(unroll):
        idx = n * unroll + u
        ch = (nchunk - 1 - idx) if reverse else idx
        rows.append(pl.ds(pl.multiple_of(ch * c, c), c))
    return rows


def _gla_parts(q_ref, k_ref, v_ref, la_ref, of_ref, o_ref, st_ref, reverse):
    c = GLA_CHUNK
    ri = lax.broadcasted_iota(jnp.int32, (c, 2 * c), 0)
    ci = lax.broadcasted_iota(jnp.int32, (c, 2 * c), 1) % c
    causal2 = (ri <= ci) if reverse else (ri >= ci)
    cum = causal2[:, :c].astype(BF16)
    mid = c // 2 if reverse else c // 2 - 1
    last = 0 if reverse else c - 1

    def local_parts(rows_list):
        cums = [_dot(cum, jnp.concatenate(_split2(la_ref[rows, :]), axis=1)) for rows in rows_list]
        scores, tails = [], []
        for rows, parts in zip(rows_list, cums):
            qf = q_ref[rows, :].astype(F32)
            kf = k_ref[rows, :].astype(F32)
            b = parts[:, :LANES] + parts[:, LANES:]
            b_ref = b[mid:mid + 1, :]
            b_last = b[last:last + 1, :]
            q_in = (qf * jnp.exp(b - b_ref)).astype(BF16)
            k_in = (kf * jnp.exp(b_ref - b)).astype(BF16)
            scores.append(_dot_nt(q_in, _head_rows(k_in)))
            kd = _head_rows((kf * jnp.exp(b_last - b)).astype(BF16))
            decay = jnp.transpose(jnp.broadcast_to(jnp.exp(b_last), (LANES, LANES)))
            tails.append(((qf * jnp.exp(b)).astype(BF16), jnp.concatenate([decay, decay], axis=1), kd))
        out = []
        for rows, s, (q_dec, decay, kd) in zip(rows_list, scores, tails):
            vbd = _block_diag_values(v_ref[rows, :])
            intra = _dot(jnp.where(causal2, s, 0.0).astype(BF16), vbd)
            out.append((intra, q_dec, decay, _dot_tn(kd, vbd)))
        return out

    def state_steps(rows_list, local):
        st = st_ref[...]
        for r, (intra, q_dec, decay, upd) in zip(rows_list, local):
            o = intra + _dot(q_dec, st.astype(BF16))
            o_ref[r, :] = (o + of_ref[r, :]) if reverse else o
            st = st * decay + upd
        st_ref[...] = st

    return local_parts, state_steps


def _ret_parts(q_ref, k_ref, v_ref, dm_ref, qd_ref, kd_ref, cd_ref, of_ref, o_ref, st_ref, reverse):
    def local_parts(rows_list):
        scores = [_dot_nt(q_ref[rows, :], _head_rows(k_ref[rows, :])) for rows in rows_list]
        out = []
        for rows, s in zip(rows_list, scores):
            vbd = _block_diag_values(v_ref[rows, :])
            intra = _dot((s * dm_ref[0]).astype(BF16), vbd)
            kd = (jnp.concatenate([k_ref[rows, :], k_ref[rows, :]], axis=0).astype(F32) * kd_ref[0]).astype(BF16)
            out.append((intra, q_ref[rows, :], _dot_tn(kd, vbd)))
        return out

    def state_steps(rows_list, local):
        st = st_ref[...]
        for r, (intra, qc, upd) in zip(rows_list, local):
            o = intra + _dot(qc, st.astype(BF16)) * qd_ref[0]
            o_ref[r, :] = (o + of_ref[r, :]) if reverse else o
            st = st * cd_ref[0] + upd
        st_ref[...] = st

    return local_parts, state_steps


def _scan_kernel(*refs, ts, reverse):
    if reverse:
        (gq, gk, gv, la, rq, rk, rv, dm, qd, kd, cd, gof, rof, go, ro, gst, rst) = refs
    else:
        (gq, gk, gv, la, rq, rk, rv, dm, qd, kd, cd, go, ro, gst, rst) = refs
        gof = rof = None

    @pl.when(pl.program_id(2) == 0)
    def _():
        gst[...] = jnp.zeros(gst.shape, F32)
        rst[...] = jnp.zeros(rst.shape, F32)

    g_local, g_steps = _gla_parts(gq, gk, gv, la, gof, go, gst, reverse)
    r_local, r_steps = _ret_parts(rq, rk, rv, dm, qd, kd, cd, rof, ro, rst, reverse)
    tokens = min(SCAN_UNROLL * GLA_CHUNK, ts)
    assert ts % tokens == 0 and tokens % RET_CHUNK == 0
    g_n, r_n = tokens // GLA_CHUNK, tokens // RET_CHUNK
    trips = ts // tokens

    def body(n, carry):
        g_rows = _scan_rows(n, g_n, trips * g_n, GLA_CHUNK, reverse)
        r_rows = _scan_rows(n, r_n, trips * r_n, RET_CHUNK, reverse)
        g_loc = g_local(g_rows)
        r_loc = r_local(r_rows)
        g_steps(g_rows, g_loc)
        r_steps(r_rows, r_loc)
        return carry

    lax.fori_loop(0, trips, body, 0)


def _scan_specs(batch, seq, ts, reverse):
    ns = seq // ts
    blk = (lambda i: ns - 1 - i) if reverse else (lambda i: i)
    narrow = pl.BlockSpec((ts, LANES), lambda b, p, i: (b * ns + blk(i), p))
    wide = pl.BlockSpec((ts, 2 * GLA_DV), lambda b, p, i: (b * ns + blk(i), p))
    return ns, narrow, wide


def _ret_tables(reverse):
    c = RET_CHUNK
    lg = jnp.log(1.0 - 2.0 ** (-5.0 - jnp.arange(RET_HEADS, dtype=F32)))
    idx = jnp.arange(c, dtype=F32)
    if reverse:
        lg = lg[::-1]
        diff = idx[None, :] - idx[:, None]
        q_pow, k_pow = c - idx, idx
    else:
        diff = idx[:, None] - idx[None, :]
        q_pow, k_pow = idx + 1.0, c - 1.0 - idx
    lgh = lg[:, None, None]
    dm = jnp.where(diff >= 0, jnp.exp(lgh * jnp.maximum(diff, 0.0)), 0.0)
    qd = jnp.exp(lg[:, None] * q_pow[None, :])
    kd = jnp.exp(lg[:, None] * k_pow[None, :])
    cd = jnp.exp(lg * c)
    npair = RET_HEADS // 2
    dm = dm.reshape(npair, 2, c, c).transpose(0, 2, 1, 3).reshape(npair, c, 2 * c)
    qd = jnp.repeat(qd.reshape(npair, 2, c).transpose(0, 2, 1), RET_DV, axis=2)
    own = jnp.arange(LANES)[None, :] // RET_DK == jnp.arange(2)[:, None]
    kd = jnp.where(own[None, :, None, :], kd.reshape(npair, 2, c, 1), 0.0).reshape(npair, 2 * c, LANES)
    cd = jnp.broadcast_to(jnp.repeat(cd.reshape(npair, 2), RET_DK, axis=1)[:, :, None], (npair, LANES, 2 * RET_DV))
    return dm, qd, kd, cd


def _scans(gq, gk, gv, la_f, la_b, rq, rk, rv, batch, seq):
    t = gq.shape[0]
    ts = min(SCAN_TILE, seq)
    c = RET_CHUNK
    params = pltpu.CompilerParams(dimension_semantics=("parallel", "parallel", "arbitrary"),
                                  vmem_limit_bytes=VMEM_LIMIT)
    scratch = [pltpu.VMEM((LANES, 2 * GLA_DV), F32), pltpu.VMEM((LANES, 2 * RET_DV), F32)]
    out_shape = [jax.ShapeDtypeStruct((t, GLA_HEADS * GLA_DV), F32), jax.ShapeDtypeStruct((t, RET_HEADS * RET_DV), F32)]
    tab_specs = [pl.BlockSpec((1, c, 2 * c), lambda b, p, i: (p, 0, 0)),
                 pl.BlockSpec((1, c, 2 * RET_DV), lambda b, p, i: (p, 0, 0)),
                 pl.BlockSpec((1, 2 * c, LANES), lambda b, p, i: (p, 0, 0)),
                 pl.BlockSpec((1, LANES, 2 * RET_DV), lambda b, p, i: (p, 0, 0))]
    ns, narrow, wide = _scan_specs(batch, seq, ts, False)
    g_f, r_f = pl.pallas_call(
        functools.partial(_scan_kernel, ts=ts, reverse=False),
        grid=(batch, GLA_HEADS // 2, ns),
        in_specs=[narrow, narrow, wide, narrow, narrow, narrow, wide] + tab_specs,
        out_specs=[wide, wide], out_shape=out_shape,
        scratch_shapes=scratch, compiler_params=params, name="scan_fwd",
    )(gq, gk, gv, la_f, rq, rk, rv, *_ret_tables(False))
    ns, narrow, wide = _scan_specs(batch, seq, ts, True)
    return pl.pallas_call(
        functools.partial(_scan_kernel, ts=ts, reverse=True),
        grid=(batch, GLA_HEADS // 2, ns),
        in_specs=[narrow, narrow, wide, narrow, narrow, narrow, wide] + tab_specs + [wide, wide],
        out_specs=[wide, wide], out_shape=out_shape,
        scratch_shapes=scratch, compiler_params=params, name="scan_bwd",
    )(gq, gk, gv, la_b, rq, rk, rv, *_ret_tables(True), g_f, r_f)


def _merge_mlp_kernel(x_ref, oa_ref, og_ref, gg_ref, or_ref, rg_ref, gmix_ref, gmlp_ref, gfin_ref, ggain_ref,
                      wm_ref, wb_ref, wo_ref, wu_ref, wd_ref, y_ref, *, final):
    x = x_ref[...]
    xn = _rms(x, gmix_ref[...]).astype(BF16)

    def per_head(o_ref, gate_ref, norm):
        heads = []
        for h in range(BRANCH_W // LANES):
            sl = slice(h * LANES, (h + 1) * LANES)
            heads.append((norm(o_ref[:, sl]) * gate_ref[:, sl].astype(F32)).astype(BF16))
        return jnp.concatenate(heads, axis=1)

    def group_norm(o):
        xc = o - jnp.mean(o, axis=-1, keepdims=True)
        return xc * lax.rsqrt(jnp.mean(xc * xc, axis=-1, keepdims=True) + EPS)

    branches = (oa_ref[...],
                per_head(og_ref, gg_ref, lambda o: _rms(o, ggain_ref[...])),
                per_head(or_ref, rg_ref, group_norm))
    merged = None
    for i, br in enumerate(branches):
        gate = jax.nn.sigmoid(_dot(xn, wm_ref[:, i * D_MODEL:(i + 1) * D_MODEL]))
        term = gate * _dot(br, wb_ref[i])
        merged = term if merged is None else merged + term
    h = x + _dot(merged.astype(BF16), wo_ref[...])
    hn = _rms(h, gmlp_ref[...]).astype(BF16)
    y = h
    for j in range(D_FF // D_MODEL):
        u = jnp.maximum(_dot(hn, wu_ref[:, j * D_MODEL:(j + 1) * D_MODEL]), 0.0)
        y = y + _dot((u * u).astype(BF16), wd_ref[j * D_MODEL:(j + 1) * D_MODEL, :])
    if final:
        y = _rms(y, gfin_ref[...])
    y_ref[...] = y


def _merge_mlp(x2d, oa, og, gg, o_r, rg, gmix, gmlp, gfin, ggain, wm, wb, wo, wu, wd, final):
    t = x2d.shape[0]
    tm = min(TOKEN_TILE, t)
    row = lambda w: pl.BlockSpec((tm, w), lambda i: (i, 0))
    vec = _const_spec((1, D_MODEL))
    return pl.pallas_call(
        functools.partial(_merge_mlp_kernel, final=final),
        grid=(t // tm,),
        in_specs=[row(D_MODEL)] + [row(BRANCH_W)] * 5 + [vec, vec, vec, _const_spec((1, GLA_DV)),
                  _const_spec(wm.shape), _const_spec(wb.shape), _const_spec(wo.shape),
                  _const_spec(wu.shape), _const_spec(wd.shape)],
        out_specs=row(D_MODEL),
        out_shape=jax.ShapeDtypeStruct((t, D_MODEL), F32),
        compiler_params=pltpu.CompilerParams(dimension_semantics=("parallel",), vmem_limit_bytes=VMEM_LIMIT),
        name="merge_mlp",
    )(x2d, oa, og, gg, o_r, rg, gmix, gmlp, gfin, ggain, wm, wb, wo, wu, wd)


def _rope_tables(seq):
    pos = jnp.arange(seq)
    axis_dim = HEAD_DIM // 2
    inv = ROPE_THETA ** (-jnp.arange(0, axis_dim, 2, dtype=F32) / axis_dim)
    ang_r = (pos // GRID_W).astype(F32)[:, None] * inv[None, :]
    ang_c = (pos % GRID_W).astype(F32)[:, None] * inv[None, :]
    ang = jnp.concatenate([ang_r, ang_r, ang_c, ang_c], axis=-1)
    ang = jnp.concatenate([ang, ang], axis=-1)
    first = (jnp.arange(LANES) % axis_dim) < axis_dim // 2
    cos, sin = jnp.cos(ang), jnp.sin(ang)
    return cos, jnp.where(first, -sin, 0.0), jnp.where(first, 0.0, sin)


def _prep_layer(w_in, gla_w_gate, gla_b_gate, q_gain, k_gain, gla_gain, w_branch, w_merge, w_out, w_up, w_down):
    splits = (512, 128, 128, 256, 256, 512, 32, 512, 256, 256, 512, 512)
    names = ("aq", "ak", "av", "gq", "gk", "gv", "ga", "gg", "rq", "rk", "rv", "rg")
    w_in = w_in.astype(BF16)
    cols, start = {}, 0
    for n, w in zip(names, splits):
        cols[n] = w_in[:, start:start + w]
        start += w
    cols["ga"] = jnp.pad(cols["ga"], ((0, 0), (0, _W_GA - 2 * GLA_GATE_RANK)))
    order = sorted(_OFF, key=lambda n: _OFF[n][0])
    w_in_r = jnp.concatenate([cols[n] for n in order], axis=1)
    nk = GLA_HEADS * GLA_DK
    r = GLA_GATE_RANK
    wg = jnp.zeros((_W_GA, 2 * nk), F32)
    wg = wg.at[:r, :nk].set(gla_w_gate[0]).at[r:2 * r, nk:].set(gla_w_gate[1]).astype(BF16)
    bg = jnp.concatenate([gla_b_gate[0], gla_b_gate[1]])[None, :]
    two = lambda g: jnp.concatenate([g, g])[None, :]
    att_shift = jnp.ceil(1.02 * ATT_Q_SCALE * HEAD_DIM * jnp.max(jnp.abs(q_gain)) * jnp.max(jnp.abs(k_gain)))
    return dict(w_in=w_in_r, wg=wg, bg=bg, qg=two(q_gain), kg=two(k_gain), gla_gain=gla_gain[None, :],
                att_shift=att_shift.reshape(1),
                wb=w_branch.astype(BF16), wm=w_merge.astype(BF16), wo=w_out.astype(BF16),
                wu=w_up.astype(BF16), wd=w_down.astype(BF16))


def _trunk(x, layers, norm_mix, norm_mlp, norm_final):
    batch, seq, _ = x.shape
    x2d = x.reshape(batch * seq, D_MODEL)
    cos, sa, sb = _rope_tables(seq)
    depth = len(layers)
    for l, w in enumerate(layers):
        gmix = norm_mix[l][None, :]
        (aq, ak, av, gq, gk, gv, gg, rq, rk, rv, rg, la_f, la_b) = _proj(
            x2d, seq, gmix, w["w_in"], w["wg"], w["bg"], w["qg"], w["kg"], cos, sa, sb)
        o_att = _attention(aq, ak, av, w["att_shift"], batch, seq)
        o_gla, o_ret = _scans(gq, gk, gv, la_f, la_b, rq, rk, rv, batch, seq)
        x2d = _merge_mlp(x2d, o_att, o_gla, gg, o_ret, rg, gmix, norm_mlp[l][None, :], norm_final[None, :],
                         w["gla_gain"], w["wm"], w["wb"], w["wo"], w["wu"], w["wd"], final=(l == depth - 1))
    return x2d.reshape(batch, seq, D_MODEL)


def kernel(x_prompt, x_sample, norm_mix, norm_mlp, w_in, attn_q_norm, attn_k_norm, gla_w_gate, gla_b_gate,
           gla_out_norm, w_branch, w_merge, w_out, w_up, w_down, norm_final):
    layers = [_prep_layer(w_in[l], gla_w_gate[l], gla_b_gate[l], attn_q_norm[l], attn_k_norm[l],
                          gla_out_norm[l], w_branch[l], w_merge[l], w_out[l], w_up[l], w_down[l])
              for l in range(w_in.shape[0])]
    y_prompt = _trunk(x_prompt, layers, norm_mix, norm_mlp, norm_final)
    y_sample = _trunk(x_sample, layers, norm_mix, norm_mlp, norm_final)
    return (y_prompt, y_sample)
```

```python
import functools
import math

import jax
import jax.numpy as jnp
from jax import lax
from jax.experimental import pallas as pl
from jax.experimental.pallas import tpu as pltpu

F32 = jnp.float32
BF16 = jnp.bfloat16

D_MODEL = 1024
GRID_W = 64
HEAD_DIM = 64
ATT_HEADS = 8
ATT_KV_HEADS = 2
ROPE_THETA = 10000.0
GLA_HEADS = 4
GLA_DK = 64
GLA_DV = 128
GLA_GATE_RANK = 16
GLA_GATE_TAU = 16.0
GLA_CHUNK = 64
RET_HEADS = 4
RET_DK = 64
RET_DV = 128
RET_CHUNK = 128
BRANCH_W = 512
N_BRANCH = 3
D_FF = 4 * D_MODEL
EPS = 1e-6

ATT_GROUPS = ATT_HEADS // ATT_KV_HEADS
ATT_Q_SCALE = HEAD_DIM ** -0.5 * math.log2(math.e)
ATT_MAX_SHIFT = 40.0
SCAN_UNROLL = 16

LANES = 128
VMEM_LIMIT = 56 * 1024 * 1024

TOKEN_TILE = 512
PROJ_TILE = 1024
ATT_Q_TILE = 1024
ATT_K_TILE = 512
SCAN_TILE = 2048

_W_AQ, _W_AK, _W_AV = 512, 128, 128
_W_GQ, _W_GK, _W_GV, _W_GG = 256, 256, 512, 512
_W_RQ, _W_RK, _W_RV, _W_RG = 256, 256, 512, 512
_W_GA = LANES
_OFF = {}
_o = 0
for _n, _w in (("aq", _W_AQ), ("ak", _W_AK), ("av", _W_AV), ("gq", _W_GQ), ("gk", _W_GK), ("gv", _W_GV),
               ("gg", _W_GG), ("rq", _W_RQ), ("rk", _W_RK), ("rv", _W_RV), ("rg", _W_RG), ("ga", _W_GA)):
    _OFF[_n] = (_o, _w)
    _o += _w
IN_WIDTH_PADDED = _o


def _const_spec(shape):
    nd = len(shape)
    return pl.BlockSpec(shape, lambda *_: (0,) * nd, pipeline_mode=pl.Buffered(1))


def _dot(a, b):
    return jnp.dot(a, b, preferred_element_type=F32)


def _dot_nt(a, b):
    return lax.dot_general(a, b, (((1,), (1,)), ((), ())), preferred_element_type=F32)


def _dot_tn(a, b):
    return lax.dot_general(a, b, (((0,), (0,)), ((), ())), preferred_element_type=F32)


def _split2(x):
    hi = x.astype(BF16)
    return hi, (x - hi.astype(F32)).astype(BF16)


def _rms(x, gain):
    return x * lax.rsqrt(jnp.mean(x * x, axis=-1, keepdims=True) + EPS) * gain


def _rope(x, cos, sin_a, sin_b):
    return x * cos + pltpu.roll(x, LANES - 16, 1) * sin_a + pltpu.roll(x, 16, 1) * sin_b


def _proj_kernel(x_ref, gmix_ref, w_ref, wg_ref, bg_ref, qg_ref, kg_ref, cos_ref, sa_ref, sb_ref,
                 aq_ref, ak_ref, av_ref, gq_ref, gk_ref, gv_ref, gg_ref, rq_ref, rk_ref, rv_ref, rg_ref,
                 laf_ref, lab_ref):
    xn = _rms(x_ref[...], gmix_ref[...]).astype(BF16)
    cos, sa, sb = cos_ref[...], sa_ref[...], sb_ref[...]

    def proj(name):
        off, w = _OFF[name]
        return _dot(xn, w_ref[:, off:off + w])

    lo_half = lax.broadcasted_iota(jnp.int32, (x_ref.shape[0], LANES), 1) < HEAD_DIM

    def head_norm(t, gain):
        y = t * t
        ss_lo = jnp.sum(jnp.where(lo_half, y, 0.0), axis=-1, keepdims=True)
        ss_hi = jnp.sum(jnp.where(lo_half, 0.0, y), axis=-1, keepdims=True)
        ss = jnp.where(lo_half, ss_lo, ss_hi)
        return t * lax.rsqrt(ss * (1.0 / HEAD_DIM) + EPS) * gain

    def per_tile(p, out_ref, fn):
        for j in range(p.shape[1] // LANES):
            out_ref[:, j * LANES:(j + 1) * LANES] = fn(p[:, j * LANES:(j + 1) * LANES]).astype(out_ref.dtype)

    per_tile(proj("aq"), aq_ref, lambda t: _rope(head_norm(t, qg_ref[...]), cos, sa, sb) * ATT_Q_SCALE)
    akv = _dot(xn, w_ref[:, _OFF["ak"][0]:_OFF["av"][0] + _W_AV])
    ak_ref[...] = _rope(head_norm(akv[:, :_W_AK], kg_ref[...]), cos, sa, sb).astype(BF16)
    av_ref[:, :_W_AV] = akv[:, _W_AK:].astype(BF16)
    av_ref[:, _W_AV:] = jnp.ones((av_ref.shape[0], LANES), BF16)
    gq_ref[...] = (proj("gq") * (GLA_DK ** -0.5)).astype(BF16)
    gk_ref[...] = proj("gk").astype(BF16)
    gv_ref[...] = proj("gv").astype(BF16)
    silu = lambda t: t * jax.nn.sigmoid(t)
    gg_ref[...] = silu(proj("gg")).astype(BF16)
    per_tile(proj("rq"), rq_ref, lambda t: _rope(t, cos, sa, sb))
    per_tile(proj("rk"), rk_ref, lambda t: _rope(t, cos, sa, sb) * (RET_DK ** -0.5))
    rv_ref[...] = proj("rv").astype(BF16)
    rg_ref[...] = silu(proj("rg")).astype(BF16)
    z = proj("ga").astype(BF16)
    pre = _dot(z, wg_ref[...]) + bg_ref[...]
    la = (jnp.minimum(pre, 0.0) - jnp.log1p(jnp.exp(-jnp.abs(pre)))) * (1.0 / GLA_GATE_TAU)
    nk = GLA_HEADS * GLA_DK
    laf_ref[...] = la[:, :nk]
    lab_ref[...] = la[:, nk:]


def _proj(x2d, seq, gmix, w_in, wg, bg, qg, kg, cos, sa, sb):
    t = x2d.shape[0]
    tm = min(PROJ_TILE, seq)
    nblk = seq // tm
    row = lambda w, dt: (pl.BlockSpec((tm, w), lambda i: (i, 0)), jax.ShapeDtypeStruct((t, w), dt))
    outs = [row(_W_AQ, BF16), row(_W_AK, BF16), row(_W_AV + LANES, BF16), row(_W_GQ, BF16), row(_W_GK, BF16),
            row(_W_GV, BF16), row(_W_GG, BF16), row(_W_RQ, BF16), row(_W_RK, BF16), row(_W_RV, BF16),
            row(_W_RG, BF16), row(GLA_HEADS * GLA_DK, F32), row(GLA_HEADS * GLA_DK, F32)]
    tab = pl.BlockSpec((tm, LANES), lambda i: (i % nblk, 0))
    return pl.pallas_call(
        _proj_kernel,
        grid=(t // tm,),
        in_specs=[pl.BlockSpec((tm, D_MODEL), lambda i: (i, 0)), _const_spec((1, D_MODEL)),
                  _const_spec(w_in.shape), _const_spec(wg.shape), _const_spec(bg.shape),
                  _const_spec((1, LANES)), _const_spec((1, LANES)), tab, tab, tab],
        out_specs=[o[0] for o in outs],
        out_shape=[o[1] for o in outs],
        compiler_params=pltpu.CompilerParams(dimension_semantics=("parallel",), vmem_limit_bytes=VMEM_LIMIT),
        name="proj",
    )(x2d, gmix, w_in, wg, bg, qg, kg, cos, sa, sb)


def _stack_query_heads(q_ref, tq):
    lo_half = lax.broadcasted_iota(jnp.int32, (tq, LANES), 1) < HEAD_DIM
    qs = []
    for h in range(ATT_HEADS):
        blk = q_ref[:, (h // 2) * LANES:(h // 2 + 1) * LANES].astype(F32)
        src_half, dst_half = h % 2, h // ATT_GROUPS
        if src_half != dst_half:
            blk = pltpu.roll(blk, HEAD_DIM, 1)
        keep = lo_half if dst_half == 0 else jnp.logical_not(lo_half)
        qs.append(jnp.where(keep, blk, 0.0).astype(BF16))
    return jnp.concatenate(qs, axis=0)


def _store_heads(acc_ref, o_ref, tq):
    lo_half = lax.broadcasted_iota(jnp.int32, (tq, LANES), 1) < HEAD_DIM
    out = acc_ref[:, :LANES] / acc_ref[:, LANES:]
    for j in range(ATT_HEADS // 2):
        a = out[(2 * j) * tq:(2 * j + 1) * tq]
        b = out[(2 * j + 1) * tq:(2 * j + 2) * tq]
        if (2 * j) // ATT_GROUPS == 0:
            b = pltpu.roll(b, HEAD_DIM, 1)
        else:
            a = pltpu.roll(a, HEAD_DIM, 1)
        o_ref[:, j * LANES:(j + 1) * LANES] = jnp.where(lo_half, a, b).astype(o_ref.dtype)


def _attn_shifted_kernel(shift_ref, q_ref, k_ref, v_ref, o_ref, acc_ref, *, tq, tk, seq):
    q = _stack_query_heads(q_ref, tq)
    shift = shift_ref[0]
    acc_ref[...] = jnp.zeros(acc_ref.shape, F32)

    def body(c, carry):
        start = pl.multiple_of(c * tk, tk)
        p = jnp.exp2(_dot_nt(q, k_ref[pl.ds(start, tk), :]) - shift).astype(BF16)
        acc_ref[...] += _dot(p, v_ref[pl.ds(start, tk), :])
        return carry

    lax.fori_loop(0, seq // tk, body, 0, unroll=4)
    _store_heads(acc_ref, o_ref, tq)


def _attn_online_kernel(q_ref, k_ref, v_ref, o_ref, acc_ref, m_ref, *, tq, tk, seq):
    q = _stack_query_heads(q_ref, tq)
    m_ref[...] = jnp.full(m_ref.shape, -jnp.inf, F32)
    acc_ref[...] = jnp.zeros(acc_ref.shape, F32)

    def body(c, carry):
        start = pl.multiple_of(c * tk, tk)
        s = _dot_nt(q, k_ref[pl.ds(start, tk), :])
        m_old = m_ref[...]
        m_new = jnp.maximum(m_old, jnp.max(s, axis=-1, keepdims=True))
        p = jnp.exp2(s - m_new).astype(BF16)
        acc_ref[...] = jnp.exp2(m_old - m_new) * acc_ref[...] + _dot(p, v_ref[pl.ds(start, tk), :])
        m_ref[...] = m_new
        return carry

    lax.fori_loop(0, seq // tk, body, 0)
    _store_heads(acc_ref, o_ref, tq)


def _attention(q, k, v, shift, batch, seq):
    t = q.shape[0]
    tq = min(ATT_Q_TILE, seq)
    tk = min(ATT_K_TILE, seq)
    nq = seq // tq
    m_rows = ATT_HEADS * tq
    row_spec = pl.BlockSpec((tq, _W_AQ), lambda b, i: (b * nq + i, 0))
    k_spec = pl.BlockSpec((seq, LANES), lambda b, i: (b, 0))
    v_spec = pl.BlockSpec((seq, 2 * LANES), lambda b, i: (b, 0))
    common = dict(
        grid=(batch, nq),
        out_specs=row_spec,
        out_shape=jax.ShapeDtypeStruct((t, _W_AQ), BF16),
        compiler_params=pltpu.CompilerParams(dimension_semantics=("parallel", "parallel"),
                                             vmem_limit_bytes=VMEM_LIMIT))
    acc = pltpu.VMEM((m_rows, 2 * LANES), F32)

    def shifted(q, k, v):
        return pl.pallas_call(
            functools.partial(_attn_shifted_kernel, tq=tq, tk=tk, seq=seq),
            in_specs=[pl.BlockSpec(memory_space=pltpu.SMEM), row_spec, k_spec, v_spec],
            scratch_shapes=[acc], name="attention", **common)(shift, q, k, v)

    def online(q, k, v):
        return pl.pallas_call(
            functools.partial(_attn_online_kernel, tq=tq, tk=tk, seq=seq),
            in_specs=[row_spec, k_spec, v_spec],
            scratch_shapes=[acc, pltpu.VMEM((m_rows, 1), F32)], name="attention_online", **common)(q, k, v)

    return lax.cond(shift[0] <= ATT_MAX_SHIFT, shifted, online, q, k, v)


def _head_rows(x):
    lane_head = lax.broadcasted_iota(jnp.int32, x.shape, 1) // GLA_DK
    zero = jnp.zeros_like(x)
    return jnp.concatenate([jnp.where(lane_head == 0, x, zero), jnp.where(lane_head == 1, x, zero)], axis=0)


def _block_diag_values(v):
    first = lax.broadcasted_iota(jnp.int32, v.shape, 1) < GLA_DV
    zero = jnp.zeros_like(v)
    return jnp.concatenate([jnp.where(first, v, zero), jnp.where(first, zero, v)], axis=0)


def _scan_rows(n, unroll, nchunk, c, reverse):
    rows = []
    for u in range(unroll):
        idx = n * unroll + u
        ch = (nchunk - 1 - idx) if reverse else idx
        rows.append(pl.ds(pl.multiple_of(ch * c, c), c))
    return rows


def _gla_parts(q_ref, k_ref, v_ref, la_ref, of_ref, o_ref, st_ref, reverse):
    c = GLA_CHUNK
    ri = lax.broadcasted_iota(jnp.int32, (c, 2 * c), 0)
    ci = lax.broadcasted_iota(jnp.int32, (c, 2 * c), 1) % c
    causal2 = (ri <= ci) if reverse else (ri >= ci)
    cum = causal2[:, :c].astype(BF16)
    mid = c // 2 if reverse else c // 2 - 1
    last = 0 if reverse else c - 1

    def local_parts(rows_list):
        cums = [_dot(cum, jnp.concatenate(_split2(la_ref[rows, :]), axis=1)) for rows in rows_list]
        scores, tails = [], []
        for rows, parts in zip(rows_list, cums):
            qf = q_ref[rows, :].astype(F32)
            kf = k_ref[rows, :].astype(F32)
            b = parts[:, :LANES] + parts[:, LANES:]
            b_ref = b[mid:mid + 1, :]
            b_last = b[last:last + 1, :]
            q_in = (qf * jnp.exp(b - b_ref)).astype(BF16)
            k_in = (kf * jnp.exp(b_ref - b)).astype(BF16)
            scores.append(_dot_nt(q_in, _head_rows(k_in)))
            kd = _head_rows((kf * jnp.exp(b_last - b)).astype(BF16))
            decay = jnp.transpose(jnp.broadcast_to(jnp.exp(b_last), (LANES, LANES)))
            tails.append(((qf * jnp.exp(b)).astype(BF16), jnp.concatenate([decay, decay], axis=1), kd))
        out = []
        for rows, s, (q_dec, decay, kd) in zip(rows_list, scores, tails):
            vbd = _block_diag_values(v_ref[rows, :])
            intra = _dot(jnp.where(causal2, s, 0.0).astype(BF16), vbd)
            out.append((intra, q_dec, decay, _dot_tn(kd, vbd)))
        return out

    def state_steps(rows_list, local):
        st = st_ref[...]
        for r, (intra, q_dec, decay, upd) in zip(rows_list, local):
            o = intra + _dot(q_dec, st.astype(BF16))
            o_ref[r, :] = (o + of_ref[r, :]) if reverse else o
            st = st * decay + upd
        st_ref[...] = st

    return local_parts, state_steps


def _ret_parts(q_ref, k_ref, v_ref, dm_ref, qd_ref, kd_ref, cd_ref, of_ref, o_ref, st_ref, reverse):
    def local_parts(rows_list):
        scores = [_dot_nt(q_ref[rows, :], _head_rows(k_ref[rows, :])) for rows in rows_list]
        out = []
        for rows, s in zip(rows_list, scores):
            vbd = _block_diag_values(v_ref[rows, :])
            intra = _dot((s * dm_ref[0]).astype(BF16), vbd)
            kd = (jnp.concatenate([k_ref[rows, :], k_ref[rows, :]], axis=0).astype(F32) * kd_ref[0]).astype(BF16)
            out.append((intra, q_ref[rows, :], _dot_tn(kd, vbd)))
        return out

    def state_steps(rows_list, local):
        st = st_ref[...]
        for r, (intra, qc, upd) in zip(rows_list, local):
            o = intra + _dot(qc, st.astype(BF16)) * qd_ref[0]
            o_ref[r, :] = (o + of_ref[r, :]) if reverse else o
            st = st * cd_ref[0] + upd
        st_ref[...] = st

    return local_parts, state_steps


def _scan_kernel(*refs, ts, reverse):
    if reverse:
        (gq, gk, gv, la, rq, rk, rv, dm, qd, kd, cd, gof, rof, go, ro, gst, rst) = refs
    else:
        (gq, gk, gv, la, rq, rk, rv, dm, qd, kd, cd, go, ro, gst, rst) = refs
        gof = rof = None

    @pl.when(pl.program_id(2) == 0)
    def _():
        gst[...] = jnp.zeros(gst.shape, F32)
        rst[...] = jnp.zeros(rst.shape, F32)

    g_local, g_steps = _gla_parts(gq, gk, gv, la, gof, go, gst, reverse)
    r_local, r_steps = _ret_parts(rq, rk, rv, dm, qd, kd, cd, rof, ro, rst, reverse)
    tokens = min(SCAN_UNROLL * GLA_CHUNK, ts)
    assert ts % tokens == 0 and tokens % RET_CHUNK == 0
    g_n, r_n = tokens // GLA_CHUNK, tokens // RET_CHUNK
    trips = ts // tokens

    def body(n, carry):
        g_rows = _scan_rows(n, g_n, trips * g_n, GLA_CHUNK, reverse)
        r_rows = _scan_rows(n, r_n, trips * r_n, RET_CHUNK, reverse)
        g_loc = g_local(g_rows)
        r_loc = r_local(r_rows)
        g_steps(g_rows, g_loc)
        r_steps(r_rows, r_loc)
        return carry

    lax.fori_loop(0, trips, body, 0)


def _scan_specs(batch, seq, ts, reverse):
    ns = seq // ts
    blk = (lambda i: ns - 1 - i) if reverse else (lambda i: i)
    narrow = pl.BlockSpec((ts, LANES), lambda b, p, i: (b * ns + blk(i), p))
    wide = pl.BlockSpec((ts, 2 * GLA_DV), lambda b, p, i: (b * ns + blk(i), p))
    return ns, narrow, wide


def _ret_tables(reverse):
    c = RET_CHUNK
    lg = jnp.log(1.0 - 2.0 ** (-5.0 - jnp.arange(RET_HEADS, dtype=F32)))
    idx = jnp.arange(c, dtype=F32)
    if reverse:
        lg = lg[::-1]
        diff = idx[None, :] - idx[:, None]
        q_pow, k_pow = c - idx, idx
    else:
        diff = idx[:, None] - idx[None, :]
        q_pow, k_pow = idx + 1.0, c - 1.0 - idx
    lgh = lg[:, None, None]
    dm = jnp.where(diff >= 0, jnp.exp(lgh * jnp.maximum(diff, 0.0)), 0.0)
    qd = jnp.exp(lg[:, None] * q_pow[None, :])
    kd = jnp.exp(lg[:, None] * k_pow[None, :])
    cd = jnp.exp(lg * c)
    npair = RET_HEADS // 2
    dm = dm.reshape(npair, 2, c, c).transpose(0, 2, 1, 3).reshape(npair, c, 2 * c)
    qd = jnp.repeat(qd.reshape(npair, 2, c).transpose(0, 2, 1), RET_DV, axis=2)
    own = jnp.arange(LANES)[None, :] // RET_DK == jnp.arange(2)[:, None]
    kd = jnp.where(own[None, :, None, :], kd.reshape(npair, 2, c, 1), 0.0).reshape(npair, 2 * c, LANES)
    cd = jnp.broadcast_to(jnp.repeat(cd.reshape(npair, 2), RET_DK, axis=1)[:, :, None], (npair, LANES, 2 * RET_DV))
    return dm, qd, kd, cd


def _scans(gq, gk, gv, la_f, la_b, rq, rk, rv, batch, seq):
    t = gq.shape[0]
    ts = min(SCAN_TILE, seq)
    c = RET_CHUNK
    params = pltpu.CompilerParams(dimension_semantics=("parallel", "parallel", "arbitrary"),
                                  vmem_limit_bytes=VMEM_LIMIT)
    scratch = [pltpu.VMEM((LANES, 2 * GLA_DV), F32), pltpu.VMEM((LANES, 2 * RET_DV), F32)]
    out_shape = [jax.ShapeDtypeStruct((t, GLA_HEADS * GLA_DV), F32), jax.ShapeDtypeStruct((t, RET_HEADS * RET_DV), F32)]
    tab_specs = [pl.BlockSpec((1, c, 2 * c), lambda b, p, i: (p, 0, 0)),
                 pl.BlockSpec((1, c, 2 * RET_DV), lambda b, p, i: (p, 0, 0)),
                 pl.BlockSpec((1, 2 * c, LANES), lambda b, p, i: (p, 0, 0)),
                 pl.BlockSpec((1, LANES, 2 * RET_DV), lambda b, p, i: (p, 0, 0))]
    ns, narrow, wide = _scan_specs(batch, seq, ts, False)
    g_f, r_f = pl.pallas_call(
        functools.partial(_scan_kernel, ts=ts, reverse=False),
        grid=(batch, GLA_HEADS // 2, ns),
        in_specs=[narrow, narrow, wide, narrow, narrow, narrow, wide] + tab_specs,
        out_specs=[wide, wide], out_shape=out_shape,
        scratch_shapes=scratch, compiler_params=params, name="scan_fwd",
    )(gq, gk, gv, la_f, rq, rk, rv, *_ret_tables(False))
    ns, narrow, wide = _scan_specs(batch, seq, ts, True)
    return pl.pallas_call(
        functools.partial(_scan_kernel, ts=ts, reverse=True),
        grid=(batch, GLA_HEADS // 2, ns),
        in_specs=[narrow, narrow, wide, narrow, narrow, narrow, wide] + tab_specs + [wide, wide],
        out_specs=[wide, wide], out_shape=out_shape,
        scratch_shapes=scratch, compiler_params=params, name="scan_bwd",
    )(gq, gk, gv, la_b, rq, rk, rv, *_ret_tables(True), g_f, r_f)


def _merge_mlp_kernel(x_ref, oa_ref, og_ref, gg_ref, or_ref, rg_ref, gmix_ref, gmlp_ref, gfin_ref, ggain_ref,
                      wm_ref, wb_ref, wo_ref, wu_ref, wd_ref, y_ref, *, final):
    x = x_ref[...]
    xn = _rms(x, gmix_ref[...]).astype(BF16)

    def per_head(o_ref, gate_ref, norm):
        heads = []
        for h in range(BRANCH_W // LANES):
            sl = slice(h * LANES, (h + 1) * LANES)
            heads.append((norm(o_ref[:, sl]) * gate_ref[:, sl].astype(F32)).astype(BF16))
        return jnp.concatenate(heads, axis=1)

    def group_norm(o):
        xc = o - jnp.mean(o, axis=-1, keepdims=True)
        return xc * lax.rsqrt(jnp.mean(xc * xc, axis=-1, keepdims=True) + EPS)

    branches = (oa_ref[...],
                per_head(og_ref, gg_ref, lambda o: _rms(o, ggain_ref[...])),
                per_head(or_ref, rg_ref, group_norm))
    merged = None
    for i, br in enumerate(branches):
        gate = jax.nn.sigmoid(_dot(xn, wm_ref[:, i * D_MODEL:(i + 1) * D_MODEL]))
        term = gate * _dot(br, wb_ref[i])
        merged = term if merged is None else merged + term
    h = x + _dot(merged.astype(BF16), wo_ref[...])
    hn = _rms(h, gmlp_ref[...]).astype(BF16)
    y = h
    for j in range(D_FF // D_MODEL):
        u = jnp.maximum(_dot(hn, wu_ref[:, j * D_MODEL:(j + 1) * D_MODEL]), 0.0)
        y = y + _dot((u * u).astype(BF16), wd_ref[j * D_MODEL:(j + 1) * D_MODEL, :])
    if final:
        y = _rms(y, gfin_ref[...])
    y_ref[...] = y


def _merge_mlp(x2d, oa, og, gg, o_r, rg, gmix, gmlp, gfin, ggain, wm, wb, wo, wu, wd, final):
    t = x2d.shape[0]
    tm = min(TOKEN_TILE, t)
    row = lambda w: pl.BlockSpec((tm, w), lambda i: (i, 0))
    vec = _const_spec((1, D_MODEL))
    return pl.pallas_call(
        functools.partial(_merge_mlp_kernel, final=final),
        grid=(t // tm,),
        in_specs=[row(D_MODEL)] + [row(BRANCH_W)] * 5 + [vec, vec, vec, _const_spec((1, GLA_DV)),
                  _const_spec(wm.shape), _const_spec(wb.shape), _const_spec(wo.shape),
                  _const_spec(wu.shape), _const_spec(wd.shape)],
        out_specs=row(D_MODEL),
        out_shape=jax.ShapeDtypeStruct((t, D_MODEL), F32),
        compiler_params=pltpu.CompilerParams(dimension_semantics=("parallel",), vmem_limit_bytes=VMEM_LIMIT),
        name="merge_mlp",
    )(x2d, oa, og, gg, o_r, rg, gmix, gmlp, gfin, ggain, wm, wb, wo, wu, wd)


def _rope_tables(seq):
    pos = jnp.arange(seq)
    axis_dim = HEAD_DIM // 2
    inv = ROPE_THETA ** (-jnp.arange(0, axis_dim, 2, dtype=F32) / axis_dim)
    ang_r = (pos // GRID_W).astype(F32)[:, None] * inv[None, :]
    ang_c = (pos % GRID_W).astype(F32)[:, None] * inv[None, :]
    ang = jnp.concatenate([ang_r, ang_r, ang_c, ang_c], axis=-1)
    ang = jnp.concatenate([ang, ang], axis=-1)
    first = (jnp.arange(LANES) % axis_dim) < axis_dim // 2
    cos, sin = jnp.cos(ang), jnp.sin(ang)
    return cos, jnp.where(first, -sin, 0.0), jnp.where(first, 0.0, sin)


def _prep_layer(w_in, gla_w_gate, gla_b_gate, q_gain, k_gain, gla_gain, w_branch, w_merge, w_out, w_up, w_down):
    splits = (512, 128, 128, 256, 256, 512, 32, 512, 256, 256, 512, 512)
    names = ("aq", "ak", "av", "gq", "gk", "gv", "ga", "gg", "rq", "rk", "rv", "rg")
    w_in = w_in.astype(BF16)
    cols, start = {}, 0
    for n, w in zip(names, splits):
        cols[n] = w_in[:, start:start + w]
        start += w
    cols["ga"] = jnp.pad(cols["ga"], ((0, 0), (0, _W_GA - 2 * GLA_GATE_RANK)))
    order = sorted(_OFF, key=lambda n: _OFF[n][0])
    w_in_r = jnp.concatenate([cols[n] for n in order], axis=1)
    nk = GLA_HEADS * GLA_DK
    r = GLA_GATE_RANK
    wg = jnp.zeros((_W_GA, 2 * nk), F32)
    wg = wg.at[:r, :nk].set(gla_w_gate[0]).at[r:2 * r, nk:].set(gla_w_gate[1]).astype(BF16)
    bg = jnp.concatenate([gla_b_gate[0], gla_b_gate[1]])[None, :]
    two = lambda g: jnp.concatenate([g, g])[None, :]
    att_shift = jnp.ceil(1.02 * ATT_Q_SCALE * HEAD_DIM * jnp.max(jnp.abs(q_gain)) * jnp.max(jnp.abs(k_gain)))
    return dict(w_in=w_in_r, wg=wg, bg=bg, qg=two(q_gain), kg=two(k_gain), gla_gain=gla_gain[None, :],
                att_shift=att_shift.reshape(1),
                wb=w_branch.astype(BF16), wm=w_merge.astype(BF16), wo=w_out.astype(BF16),
                wu=w_up.astype(BF16), wd=w_down.astype(BF16))


def _trunk(x, layers, norm_mix, norm_mlp, norm_final):
    batch, seq, _ = x.shape
    x2d = x.reshape(batch * seq, D_MODEL)
    cos, sa, sb = _rope_tables(seq)
    depth = len(layers)
    for l, w in enumerate(layers):
        gmix = norm_mix[l][None, :]
        (aq, ak, av, gq, gk, gv, gg, rq, rk, rv, rg, la_f, la_b) = _proj(
            x2d, seq, gmix, w["w_in"], w["wg"], w["bg"], w["qg"], w["kg"], cos, sa, sb)
        o_att = _attention(aq, ak, av, w["att_shift"], batch, seq)
        o_gla, o_ret = _scans(gq, gk, gv, la_f, la_b, rq, rk, rv, batch, seq)
        x2d = _merge_mlp(x2d, o_att, o_gla, gg, o_ret, rg, gmix, norm_mlp[l][None, :], norm_final[None, :],
                         w["gla_gain"], w["wm"], w["wb"], w["wo"], w["wu"], w["wd"], final=(l == depth - 1))
    return x2d.reshape(batch, seq, D_MODEL)


def kernel(x_prompt, x_sample, norm_mix, norm_mlp, w_in, attn_q_norm, attn_k_norm, gla_w_gate, gla_b_gate,
           gla_out_norm, w_branch, w_merge, w_out, w_up, w_down, norm_final):
    layers = [_prep_layer(w_in[l], gla_w_gate[l], gla_b_gate[l], attn_q_norm[l], attn_k_norm[l],
                          gla_out_norm[l], w_branch[l], w_merge[l], w_out[l], w_up[l], w_down[l])
              for l in range(w_in.shape[0])]
    y_prompt = _trunk(x_prompt, layers, norm_mix, norm_mlp, norm_final)
    y_sample = _trunk(x_sample, layers, norm_mix, norm_mlp, norm_final)
    return (y_prompt, y_sample)
```
